```python
import math
import functools
import jax
import jax.numpy as jnp
from jax import lax
import numpy as np

D_MODEL = 1024
BATCH = 2
SEQ = 8192
DEPTH = 4
DEC_BATCH = 128
DEC_SEQ = 4
PAST_LEN = 2048
PAGE_SIZE = 128

HEAD_DIM = 64
H_SB = 8
H_FOX = 8
H_GLA = 4
DK_GLA = 64
DV_GLA = 128
GLA_RANK = 16
GLA_TAU = 16.0
GLA_CHUNK = 16
H_DSA = 8
H_IDX = 8
D_IDX = 64
DSA_TOPK = 256
N_BUCKETS = 32
MAX_DISTANCE = 128
D_FF = ((8 * D_MODEL // 3 + 127) // 128) * 128
D_PLE = 256
Q_BLOCK = 128
EPS = 1e-6
MACARON_WEIGHT = 0.5
FORGET_BIAS_INIT = 3.0
MIX_WIDTH = H_SB * HEAD_DIM + H_FOX * HEAD_DIM
AB_COLS = (H_SB * HEAD_DIM,) * 3 + (H_FOX * HEAD_DIM,) * 3 + (H_FOX,)
CD_COLS = (H_GLA * DK_GLA, H_GLA * DK_GLA, H_GLA * DV_GLA, GLA_RANK, H_GLA * DV_GLA,
           H_DSA * HEAD_DIM, H_DSA * HEAD_DIM, H_DSA * HEAD_DIM, H_IDX * D_IDX, D_IDX, H_IDX)
N_NORMS = 8
NG_FFN1_PRE, NG_FFN1_POST, NG_MIX_PRE, NG_MIX_POST, NG_FFN2_PRE, NG_FFN2_POST, NG_PLE_PRE, NG_PLE_POST = range(8)

kernel_name = 'hybrid_sb_fox_gla_dsa_decode_step'


def rmsnorm(x, g):
    xf = x.astype(jnp.float32)
    y = xf * lax.rsqrt(jnp.mean(xf * xf, axis=-1, keepdims=True) + EPS)
    return (y * g.astype(jnp.float32)).astype(x.dtype)


def split_cols(h, sizes):
    return jnp.split(h, np.cumsum(sizes)[:-1].tolist(), axis=-1)


def heads(t, n):
    return t.reshape(t.shape[:-1] + (n, -1))


def swiglu(x, w_gate, w_up, w_down):
    return (jax.nn.silu(x @ w_gate) * (x @ w_up)) @ w_down


def t5_bucket(dist):
    max_exact = N_BUCKETS // 2
    d = jnp.maximum(dist, 0)
    large = max_exact + (jnp.log(jnp.maximum(d, 1).astype(jnp.float32) / max_exact)
                         / math.log(MAX_DISTANCE / max_exact) * (N_BUCKETS - max_exact)).astype(jnp.int32)
    return jnp.where(d < max_exact, d, jnp.minimum(large, N_BUCKETS - 1))


def map_query_blocks(fn, q_arrays):
    B, T = q_arrays[0].shape[:2]
    qb = min(Q_BLOCK, T)
    nb = T // qb
    blocks = tuple(a.reshape((B, nb, qb) + a.shape[2:]).swapaxes(0, 1) for a in q_arrays)
    starts = jnp.arange(nb, dtype=jnp.int32) * qb
    out = lax.map(lambda args: fn(*args[:-1], args[-1] + jnp.arange(qb, dtype=jnp.int32)), blocks + (starts,))
    return out.swapaxes(0, 1).reshape((B, T) + out.shape[3:])


def stick_breaking_attend(q, k, v, q_pos, k_pos):
    z = jnp.einsum('bqhd,bkhd->bhqk', q, k).astype(jnp.float32) * HEAD_DIM ** -0.5
    mask = k_pos[None, :] < q_pos[:, None]
    log_1m = jnp.where(mask, jax.nn.log_sigmoid(-z), 0.0)
    after = lax.cumsum(log_1m, axis=3, reverse=True) - log_1m
    w = jnp.where(mask, jnp.exp(jax.nn.log_sigmoid(z) + after), 0.0)
    return jnp.einsum('bhqk,bkhd->bqhd', w.astype(v.dtype), v)


def forgetting_attend(q, k, v, cq, ck, q_pos, k_pos):
    s = jnp.einsum('bqhd,bkhd->bhqk', q, k).astype(jnp.float32) * HEAD_DIM ** -0.5
    s = s + (cq.transpose(0, 2, 1)[:, :, :, None] - ck.transpose(0, 2, 1)[:, :, None, :])
    s = jnp.where(k_pos[None, :] <= q_pos[:, None], s, -jnp.inf)
    p = jax.nn.softmax(s, axis=-1)
    return jnp.einsum('bhqk,bkhd->bqhd', p.astype(v.dtype), v)


def gla_scan(q, k, v, log_a, s0):
    B, T = q.shape[:2]
    C = min(GLA_CHUNK, T)
    N = T // C
    chunks = tuple(t.reshape((B, N, C) + t.shape[2:]).swapaxes(0, 1) for t in (q, k, v, log_a))
    causal = jnp.tril(jnp.ones((C, C), dtype=bool))

    def step(S, inp):
        qi, ki, vi, gi = (t.astype(jnp.float32) for t in inp)
        b = jnp.cumsum(gi, axis=1)
        o_inter = jnp.einsum('bchk,bhkv->bchv', qi * jnp.exp(b), S)
        diff = jnp.where(causal[None, :, :, None, None], b[:, :, None] - b[:, None], -jnp.inf)
        att = jnp.einsum('bthk,bshk,btshk->bhts', qi, ki, jnp.exp(diff))
        o_intra = jnp.einsum('bhts,bshv->bthv', att, vi)
        b_last = b[:, -1]
        S_new = jnp.exp(b_last)[..., None] * S + jnp.einsum('bshk,bshv->bhkv', ki * jnp.exp(b_last[:, None] - b), vi)
        return S_new, o_inter + o_intra

    S_fin, o = lax.scan(step, s0.astype(jnp.float32), chunks)
    o = o.swapaxes(0, 1).reshape(B, T, H_GLA, -1)
    return o.astype(q.dtype), S_fin


def dsa_select(iq, iw, ik, q_pos, k_pos, topk):
    sc = jnp.einsum('bqhd,bkd->bqhk', iq, ik).astype(jnp.float32)
    score = jnp.einsum('bqhk,bqh->bqk', jax.nn.relu(sc), iw.astype(jnp.float32))
    score = jnp.where((k_pos[None, :] <= q_pos[:, None])[None], score, -jnp.inf)
    _, idx = lax.top_k(score, topk)
    return idx


def dsa_attend(q, k_sel, v_sel, sel_pos, q_pos, rel_bias):
    s = jnp.einsum('bqhd,bqkhd->bhqk', q, k_sel).astype(jnp.float32) * HEAD_DIM ** -0.5
    dist = q_pos[None, :, None] - sel_pos
    s = s + rel_bias[t5_bucket(dist)].astype(jnp.float32).transpose(0, 3, 1, 2)
    s = jnp.where((dist >= 0)[:, None], s, -jnp.inf)
    p = jax.nn.softmax(s, axis=-1)
    return jnp.einsum('bhqk,bqkhd->bqhd', p.astype(v_sel.dtype), v_sel)


def ab_project(h, w_in, b_forget):
    q_sb, k_sb, v_sb, q_fx, k_fx, v_fx, f_pre = split_cols(h @ w_in, AB_COLS)
    sb = tuple(heads(t, H_SB) for t in (q_sb, k_sb, v_sb))
    fx = tuple(heads(t, H_FOX) for t in (q_fx, k_fx, v_fx))
    logf = jax.nn.log_sigmoid((f_pre + b_forget).astype(jnp.float32))
    return sb, fx, logf


def ab_mix_prompt(h, w_in, b_forget):
    B, T, _ = h.shape
    (q_sb, k_sb, v_sb), (q_fx, k_fx, v_fx), logf = ab_project(h, w_in, b_forget)
    pos = jnp.arange(T, dtype=jnp.int32)
    cum = jnp.cumsum(logf, axis=1)
    o_sb = map_query_blocks(lambda qb, qp: stick_breaking_attend(qb, k_sb, v_sb, qp, pos), (q_sb,))
    o_fx = map_query_blocks(lambda qb, cb, qp: forgetting_attend(qb, k_fx, v_fx, cb, cum, qp, pos), (q_fx, cum))
    mixed = jnp.concatenate([o_sb.reshape(B, T, -1), o_fx.reshape(B, T, -1)], axis=-1)
    return mixed, (jnp.stack([k_sb, v_sb], axis=2), jnp.stack([k_fx, v_fx], axis=2), logf.astype(h.dtype))


def ab_mix_sample(h, w_in, b_forget, sb_pool, fx_pool, logf_pool, layer, page_table):
    DB, T, _ = h.shape
    (q_sb, k_sb, v_sb), (q_fx, k_fx, v_fx), logf = ab_project(h, w_in, b_forget)
    past_sb = sb_pool[layer, page_table].reshape(DB, -1, 2, H_SB, HEAD_DIM)
    past_fx = fx_pool[layer, page_table].reshape(DB, -1, 2, H_FOX, HEAD_DIM)
    past_logf = logf_pool[layer, page_table].reshape(DB, -1, H_FOX).astype(jnp.float32)
    P = past_sb.shape[1]
    q_pos = P + jnp.arange(T, dtype=jnp.int32)
    k_pos = jnp.arange(P + T, dtype=jnp.int32)
    o_sb = stick_breaking_attend(q_sb, jnp.concatenate([past_sb[:, :, 0], k_sb], 1),
                                 jnp.concatenate([past_sb[:, :, 1], v_sb], 1), q_pos, k_pos)
    cum_all = jnp.cumsum(jnp.concatenate([past_logf, logf], axis=1), axis=1)
    o_fx = forgetting_attend(q_fx, jnp.concatenate([past_fx[:, :, 0], k_fx], 1),
                             jnp.concatenate([past_fx[:, :, 1], v_fx], 1), cum_all[:, P:], cum_all, q_pos, k_pos)
    mixed = jnp.concatenate([o_sb.reshape(DB, T, -1), o_fx.reshape(DB, T, -1)], axis=-1)
    return mixed, (jnp.stack([k_sb, v_sb], axis=2), jnp.stack([k_fx, v_fx], axis=2), logf.astype(logf_pool.dtype))


def cd_project(h, w_in, gla_w_gate_up, gla_b_gate):
    q_g, k_g, v_g, g_lr, r_g, q_d, k_d, v_d, iq, ik, iw = split_cols(h @ w_in, CD_COLS)
    log_a = jax.nn.log_sigmoid((g_lr @ gla_w_gate_up + gla_b_gate).astype(jnp.float32)) / GLA_TAU
    gla = (heads(q_g, H_GLA) * DK_GLA ** -0.5, heads(k_g, H_GLA), heads(v_g, H_GLA),
           heads(log_a, H_GLA), heads(r_g, H_GLA))
    dsa = (heads(q_d, H_DSA), heads(k_d, H_DSA), heads(v_d, H_DSA))
    idx = (heads(iq, H_IDX), ik, iw.astype(jnp.float32) * H_IDX ** -0.5)
    return gla, dsa, idx


def gla_output(o, r, gla_norm):
    B, T = o.shape[:2]
    return (rmsnorm(o, gla_norm) * jax.nn.silu(r)).reshape(B, T, -1)


def cd_mix_prompt(h, w_in, gla_w_gate_up, gla_b_gate, gla_norm, rel_bias):
    B, T, _ = h.shape
    (gq, gk, gv, la, r), (q, k, v), (iq, ik, iw) = cd_project(h, w_in, gla_w_gate_up, gla_b_gate)
    o_gla, s_fin = gla_scan(gq, gk, gv, la, jnp.zeros((B, H_GLA, DK_GLA, DV_GLA), jnp.float32))
    kv = jnp.stack([k, v], axis=2)
    pos = jnp.arange(T, dtype=jnp.int32)
    topk = min(DSA_TOPK, T // 4)

    def block(qb, iqb, iwb, qp):
        idx = dsa_select(iqb, iwb, ik, qp, pos, topk)
        kv_sel = jax.vmap(lambda rows, ii: rows[ii])(kv, idx)
        return dsa_attend(qb, kv_sel[:, :, :, 0], kv_sel[:, :, :, 1], idx, qp, rel_bias)

    o_dsa = map_query_blocks(block, (q, iq, iw))
    mixed = jnp.concatenate([gla_output(o_gla, r, gla_norm), o_dsa.reshape(B, T, -1)], axis=-1)
    return mixed, (s_fin.astype(h.dtype), kv, ik)


def cd_mix_sample(h, w_in, gla_w_gate_up, gla_b_gate, gla_norm, rel_bias, gla_state, dsa_pool, idxk_pool, layer, page_table):
    DB, T, _ = h.shape
    (gq, gk, gv, la, r), (q, k, v), (iq, ik, iw) = cd_project(h, w_in, gla_w_gate_up, gla_b_gate)
    o_gla, s_fin = gla_scan(gq, gk, gv, la, gla_state[layer])
    past_ik = idxk_pool[layer, page_table].reshape(DB, -1, D_IDX)
    P = past_ik.shape[1]
    q_pos = P + jnp.arange(T, dtype=jnp.int32)
    k_pos = jnp.arange(P + T, dtype=jnp.int32)
    topk = min(DSA_TOPK, (P + T) // 4)
    idx = dsa_select(iq, iw, jnp.concatenate([past_ik, ik], axis=1), q_pos, k_pos, topk)
    kv_new = jnp.stack([k, v], axis=2)
    pidx = jnp.minimum(idx, P - 1)
    pages = jax.vmap(lambda pt, ii: pt[ii])(page_table, pidx // PAGE_SIZE)
    past_rows = dsa_pool[layer, pages, pidx % PAGE_SIZE]
    new_rows = jax.vmap(lambda rows, ii: rows[ii])(kv_new, jnp.clip(idx - P, 0, T - 1))
    kv_sel = jnp.where((idx < P)[..., None, None, None], past_rows, new_rows)
    o_dsa = dsa_attend(q, kv_sel[:, :, :, 0], kv_sel[:, :, :, 1], idx, q_pos, rel_bias)
    mixed = jnp.concatenate([gla_output(o_gla, r, gla_norm), o_dsa.reshape(DB, T, -1)], axis=-1)
    return mixed, (s_fin.astype(gla_state.dtype), kv_new, ik)


def apply_layer(x, p_i, mix_fn, g, w_gate, w_up, w_down, w_o, w_pp, w_pg):
    x = x + MACARON_WEIGHT * rmsnorm(swiglu(rmsnorm(x, g[NG_FFN1_PRE]), w_gate[0], w_up[0], w_down[0]), g[NG_FFN1_POST])
    mixed, new_state = mix_fn(rmsnorm(x, g[NG_MIX_PRE]))
    x = x + rmsnorm(mixed @ w_o, g[NG_MIX_POST])
    x = x + MACARON_WEIGHT * rmsnorm(swiglu(rmsnorm(x, g[NG_FFN2_PRE]), w_gate[1], w_up[1], w_down[1]), g[NG_FFN2_POST])
    gate = jax.nn.sigmoid(rmsnorm(x, g[NG_PLE_PRE]) @ w_pg)
    x = x + rmsnorm((p_i @ w_pp) * gate, g[NG_PLE_POST])
    return x, new_state


def setup_inputs(seed: int = 0) -> dict:
    key = jax.random.key(seed)
    ks = jax.random.split(key, 32)

    def nrm(i, shape, scale):
        return jax.random.normal(ks[i], shape, jnp.float32) * scale

    n_ab, n_cd = (DEPTH + 1) // 2, DEPTH // 2
    n_pages = PAST_LEN // PAGE_SIZE
    n_used = DEC_BATCH * n_pages
    n_pool = n_used + max(1, n_used // 4)
    perm = jax.random.permutation(ks[0], n_pool).astype(jnp.int32)
    page_table = perm[:n_used].reshape(DEC_BATCH, n_pages)
    ab_width, cd_width = sum(AB_COLS), sum(CD_COLS)
    return {
        'x_prompt': nrm(1, (BATCH, SEQ, D_MODEL), 1.0),
        'x_sample': nrm(2, (DEC_BATCH, DEC_SEQ, D_MODEL), 1.0),
        'p_prompt': nrm(3, (DEPTH, BATCH, SEQ, D_PLE), 1.0),
        'p_sample': nrm(4, (DEPTH, DEC_BATCH, DEC_SEQ, D_PLE), 1.0),
        'cache_sb_kv': nrm(5, (n_ab, n_pool, PAGE_SIZE, 2, H_SB, HEAD_DIM), 1.0),
        'cache_fox_kv': nrm(6, (n_ab, n_pool, PAGE_SIZE, 2, H_FOX, HEAD_DIM), 1.0),
        'cache_fox_logf': jax.nn.log_sigmoid(FORGET_BIAS_INIT + nrm(7, (n_ab, n_pool, PAGE_SIZE, H_FOX), 1.0)),
        'state_gla': nrm(8, (n_cd, DEC_BATCH, H_GLA, DK_GLA, DV_GLA), 1.0),
        'cache_dsa_kv': nrm(9, (n_cd, n_pool, PAGE_SIZE, 2, H_DSA, HEAD_DIM), 1.0),
        'cache_dsa_idxk': nrm(10, (n_cd, n_pool, PAGE_SIZE, D_IDX), 1.0),
        'page_table': page_table,
        'norm_gains': 1.0 + nrm(11, (DEPTH, N_NORMS, D_MODEL), 0.05),
        'ffn_w_gate': nrm(12, (DEPTH, 2, D_MODEL, D_FF), D_MODEL ** -0.5),
        'ffn_w_up': nrm(13, (DEPTH, 2, D_MODEL, D_FF), D_MODEL ** -0.5),
        'ffn_w_down': nrm(14, (DEPTH, 2, D_FF, D_MODEL), D_FF ** -0.5),
        'w_out': nrm(15, (DEPTH, MIX_WIDTH, D_MODEL), MIX_WIDTH ** -0.5),
        'ple_w_proj': nrm(16, (DEPTH, D_PLE, D_MODEL), D_PLE ** -0.5),
        'ple_w_gate': nrm(17, (DEPTH, D_MODEL, D_MODEL), D_MODEL ** -0.5),
        'ab_w_in': nrm(18, (n_ab, D_MODEL, ab_width), D_MODEL ** -0.5),
        'ab_b_forget': FORGET_BIAS_INIT + nrm(19, (n_ab, H_FOX), 0.5),
        'cd_w_in': nrm(20, (n_cd, D_MODEL, cd_width), D_MODEL ** -0.5),
        'gla_w_gate_up': nrm(21, (n_cd, GLA_RANK, H_GLA * DK_GLA), GLA_RANK ** -0.5),
        'gla_b_gate': nrm(22, (n_cd, H_GLA * DK_GLA), 0.1),
        'gla_norm': 1.0 + nrm(23, (n_cd, DV_GLA), 0.05),
        'rel_bias': nrm(24, (N_BUCKETS, H_DSA), 0.5),
    }


def reference(x_prompt, x_sample, p_prompt, p_sample, cache_sb_kv, cache_fox_kv, cache_fox_logf,
              state_gla, cache_dsa_kv, cache_dsa_idxk, page_table, norm_gains, ffn_w_gate, ffn_w_up,
              ffn_w_down, w_out, ple_w_proj, ple_w_gate, ab_w_in, ab_b_forget, cd_w_in,
              gla_w_gate_up, gla_b_gate, gla_norm, rel_bias):
    xp, xs = x_prompt, x_sample
    sb_p, sb_s, fkv_p, fkv_s, flf_p, flf_s = [], [], [], [], [], []
    gs_p, gs_s, dkv_p, dkv_s, dik_p, dik_s = [], [], [], [], [], []
    for i in range(DEPTH):
        j = i // 2
        lw = (norm_gains[i], ffn_w_gate[i], ffn_w_up[i], ffn_w_down[i], w_out[i], ple_w_proj[i], ple_w_gate[i])
        if i % 2 == 0:
            mix_p = functools.partial(ab_mix_prompt, w_in=ab_w_in[j], b_forget=ab_b_forget[j])
            mix_s = functools.partial(ab_mix_sample, w_in=ab_w_in[j], b_forget=ab_b_forget[j],
                                      sb_pool=cache_sb_kv, fx_pool=cache_fox_kv, logf_pool=cache_fox_logf,
                                      layer=j, page_table=page_table)
            xp, (a_sb, a_fkv, a_flf) = apply_layer(xp, p_prompt[i], mix_p, *lw)
            xs, (b_sb, b_fkv, b_flf) = apply_layer(xs, p_sample[i], mix_s, *lw)
            sb_p.append(a_sb); fkv_p.append(a_fkv); flf_p.append(a_flf)
            sb_s.append(b_sb); fkv_s.append(b_fkv); flf_s.append(b_flf)
        else:
            mix_p = functools.partial(cd_mix_prompt, w_in=cd_w_in[j], gla_w_gate_up=gla_w_gate_up[j],
                                      gla_b_gate=gla_b_gate[j], gla_norm=gla_norm[j], rel_bias=rel_bias)
            mix_s = functools.partial(cd_mix_sample, w_in=cd_w_in[j], gla_w_gate_up=gla_w_gate_up[j],
                                      gla_b_gate=gla_b_gate[j], gla_norm=gla_norm[j], rel_bias=rel_bias,
                                      gla_state=state_gla, dsa_pool=cache_dsa_kv, idxk_pool=cache_dsa_idxk,
                                      layer=j, page_table=page_table)
            xp, (a_gs, a_dkv, a_dik) = apply_layer(xp, p_prompt[i], mix_p, *lw)
            xs, (b_gs, b_dkv, b_dik) = apply_layer(xs, p_sample[i], mix_s, *lw)
            gs_p.append(a_gs); dkv_p.append(a_dkv); dik_p.append(a_dik)
            gs_s.append(b_gs); dkv_s.append(b_dkv); dik_s.append(b_dik)
    return (xp, xs,
            jnp.stack(sb_p), jnp.stack(sb_s),
            jnp.stack(fkv_p), jnp.stack(fkv_s),
            jnp.stack(flf_p), jnp.stack(flf_s),
            jnp.stack(gs_p), jnp.stack(gs_s),
            jnp.stack(dkv_p), jnp.stack(dkv_s),
            jnp.stack(dik_p), jnp.stack(dik_s))
```

```python
import functools
import math

import jax
import jax.numpy as jnp
import numpy as np
from jax import lax
from jax.experimental import pallas as pl
from jax.experimental.pallas import tpu as pltpu

F32 = jnp.float32
BF16 = jnp.bfloat16
I32 = jnp.int32

HEAD_DIM = 64
H_SB = 8
H_FOX = 8
H_GLA = 4
DK_GLA = 64
DV_GLA = 128
GLA_RANK = 16
GLA_TAU = 16.0
GLA_CHUNK = 16
H_DSA = 8
H_IDX = 8
D_IDX = 64
DSA_TOPK = 256
N_BUCKETS = 32
MAX_DISTANCE = 128
EPS = 1e-6
MACARON_WEIGHT = 0.5
NG_FFN1_PRE, NG_FFN1_POST, NG_MIX_PRE, NG_MIX_POST, NG_FFN2_PRE, NG_FFN2_POST, NG_PLE_PRE, NG_PLE_POST = range(8)

LANES = 128
VMEM_LIMIT_BYTES = 56 * 1024 * 1024
NEG_BIG = -1e30
INT_MIN = -2 ** 31


def _cparams(sem):
    return pltpu.CompilerParams(dimension_semantics=sem, vmem_limit_bytes=VMEM_LIMIT_BYTES)


def _pick_tile(n, cap):
    if n <= cap:
        return n
    for t in range(cap - cap % 8, 7, -8):
        if n % t == 0:
            return t
    return n


def _rms(x, g):
    return x * lax.rsqrt(jnp.mean(x * x, axis=-1, keepdims=True) + EPS) * g


def _log_sigmoid(x):
    return jnp.minimum(x, 0.0) - jnp.log1p(jnp.exp(-jnp.abs(x)))


def _dot(a, b):
    return jnp.dot(a, b, preferred_element_type=F32)


def _dot_nt(a, b):
    return lax.dot_general(a, b, (((1,), (1,)), ((), ())), preferred_element_type=F32)


def _dot_tn(a, b):
    return lax.dot_general(a, b, (((0,), (0,)), ((), ())), preferred_element_type=F32)


def _split2(x):
    hi = x.astype(BF16)
    lo = (x - hi.astype(F32)).astype(BF16)
    return hi, lo


def _split3(x):
    hi = x.astype(BF16)
    r = x - hi.astype(F32)
    mid = r.astype(BF16)
    lo = (r - mid.astype(F32)).astype(BF16)
    return hi, mid, lo


def _iota(shape, dim):
    return lax.broadcasted_iota(I32, shape, dim)


def _ffn_kernel(x_ref, g_ref, wg_ref, wu_ref, wd_ref, o_ref, h_scr, acc_scr, *, pre, post):
    f = pl.program_id(1)

    @pl.when(f == 0)
    def _():
        h_scr[...] = _rms(x_ref[...], g_ref[pre:pre + 1, :]).astype(BF16)
        acc_scr[...] = jnp.zeros_like(acc_scr)

    h = h_scr[...]
    gate = _dot(h, wg_ref[...])
    up = _dot(h, wu_ref[...])
    act = gate * jax.nn.sigmoid(gate) * up
    acc_scr[...] += _dot(act.astype(BF16), wd_ref[...])

    @pl.when(f == pl.num_programs(1) - 1)
    def _():
        o_ref[...] = x_ref[...] + MACARON_WEIGHT * _rms(acc_scr[...], g_ref[post:post + 1, :])


def _ffn(x, g, wg, wu, wd, pre, post):
    m, d = x.shape
    ff = wg.shape[1]
    tm = _pick_tile(m, 512)
    tf = ff
    for cand in range(min(ff, 1408), 127, -128):
        if ff % cand == 0:
            tf = cand
            break
    return pl.pallas_call(
        functools.partial(_ffn_kernel, pre=pre, post=post),
        grid=(m // tm, ff // tf),
        in_specs=[
            pl.BlockSpec((tm, d), lambda i, f: (i, 0)),
            pl.BlockSpec(g.shape, lambda i, f: (0, 0)),
            pl.BlockSpec((d, tf), lambda i, f: (0, f)),
            pl.BlockSpec((d, tf), lambda i, f: (0, f)),
            pl.BlockSpec((tf, d), lambda i, f: (f, 0)),
        ],
        out_specs=pl.BlockSpec((tm, d), lambda i, f: (i, 0)),
        out_shape=jax.ShapeDtypeStruct((m, d), F32),
        scratch_shapes=[pltpu.VMEM((tm, d), BF16), pltpu.VMEM((tm, d), F32)],
        compiler_params=_cparams(("parallel", "arbitrary")),
        name="ffn",
    )(x, g, wg, wu, wd)


AB_Q_SB, AB_KV_SB, AB_Q_FX, AB_KV_FX, AB_F = 0, 512, 1536, 2048, 3072
AB_WIDTH_PAD = 3200


def _proj_ab_kernel(x_ref, g_ref, w_ref, b_ref, qsb_ref, kvsb_ref, qfx_ref, kvfx_ref, lf_ref):
    h = _rms(x_ref[...], g_ref[NG_MIX_PRE:NG_MIX_PRE + 1, :]).astype(BF16)
    qsb_ref[...] = _dot(h, w_ref[:, AB_Q_SB:AB_KV_SB])
    kvsb_ref[...] = _dot(h, w_ref[:, AB_KV_SB:AB_Q_FX])
    qfx_ref[...] = _dot(h, w_ref[:, AB_Q_FX:AB_KV_FX])
    kvfx_ref[...] = _dot(h, w_ref[:, AB_KV_FX:AB_F])
    lf_ref[...] = _log_sigmoid(_dot(h, w_ref[:, AB_F:AB_WIDTH_PAD]) + b_ref[...])


def _proj_ab(x, g, w_pad, b_pad):
    m, d = x.shape
    tm = _pick_tile(m, 512)
    widths = (512, 1024, 512, 1024, LANES)
    return pl.pallas_call(
        _proj_ab_kernel,
        grid=(m // tm,),
        in_specs=[
            pl.BlockSpec((tm, d), lambda i: (i, 0)),
            pl.BlockSpec(g.shape, lambda i: (0, 0)),
            pl.BlockSpec(w_pad.shape, lambda i: (0, 0)),
            pl.BlockSpec(b_pad.shape, lambda i: (0, 0)),
        ],
        out_specs=[pl.BlockSpec((tm, w), lambda i: (i, 0)) for w in widths],
        out_shape=[jax.ShapeDtypeStruct((m, w), F32) for w in widths],
        compiler_params=_cparams(("parallel",)),
        name="proj_ab",
    )(x, g, w_pad, b_pad)


CD_QKV_G, CD_R_G, CD_Q_D, CD_KV_D, CD_IQ, CD_IK2, CD_MISC, CD_WIDTH_PAD = 0, 1024, 1536, 2048, 3072, 3584, 3712, 3840
MISC_IW, MISC_GLR = 0, 8


def _proj_cd_kernel(x_ref, g_ref, w_ref, wgu_ref, bg_ref, qkvg_ref, rg_ref, qd_ref, kvd_ref, iq_ref, ik2_ref,
                    misc_ref, la_ref):
    h = _rms(x_ref[...], g_ref[NG_MIX_PRE:NG_MIX_PRE + 1, :]).astype(BF16)
    qkvg_ref[...] = _dot(h, w_ref[:, CD_QKV_G:CD_R_G])
    rg_ref[...] = _dot(h, w_ref[:, CD_R_G:CD_Q_D])
    qd_ref[...] = _dot(h, w_ref[:, CD_Q_D:CD_KV_D])
    kvd_ref[...] = _dot(h, w_ref[:, CD_KV_D:CD_IQ])
    iq_ref[...] = _dot(h, w_ref[:, CD_IQ:CD_IK2])
    ik2_ref[...] = _dot(h, w_ref[:, CD_IK2:CD_MISC])
    misc = _dot(h, w_ref[:, CD_MISC:CD_WIDTH_PAD])
    misc_ref[...] = misc
    la_ref[...] = _log_sigmoid(_dot(misc.astype(BF16), wgu_ref[...]) + bg_ref[...]) / GLA_TAU


def _proj_cd(x, g, w_pad, wgu_pad, bg):
    m, d = x.shape
    tm = _pick_tile(m, 512)
    widths = (1024, 512, 512, 1024, 512, LANES, LANES, H_GLA * DK_GLA)
    return pl.pallas_call(
        _proj_cd_kernel,
        grid=(m // tm,),
        in_specs=[
            pl.BlockSpec((tm, d), lambda i: (i, 0)),
            pl.BlockSpec(g.shape, lambda i: (0, 0)),
            pl.BlockSpec(w_pad.shape, lambda i: (0, 0)),
            pl.BlockSpec(wgu_pad.shape, lambda i: (0, 0)),
            pl.BlockSpec(bg.shape, lambda i: (0, 0)),
        ],
        out_specs=[pl.BlockSpec((tm, w), lambda i: (i, 0)) for w in widths],
        out_shape=[jax.ShapeDtypeStruct((m, w), F32) for w in widths],
        compiler_params=_cparams(("parallel",)),
        name="proj_cd",
    )(x, g, w_pad, wgu_pad, bg)


def _mixout_kernel(x_ref, a_ref, b_ref, g_ref, wo_ref, o_ref):
    wa = a_ref.shape[1]
    y = _dot(a_ref[...].astype(BF16), wo_ref[:wa, :]) + _dot(b_ref[...].astype(BF16), wo_ref[wa:, :])
    o_ref[...] = x_ref[...] + _rms(y, g_ref[NG_MIX_POST:NG_MIX_POST + 1, :])


def _mixout(x, mix_a, mix_b, g, wo):
    m, d = x.shape
    tm = _pick_tile(m, 512)
    return pl.pallas_call(
        _mixout_kernel,
        grid=(m // tm,),
        in_specs=[
            pl.BlockSpec((tm, d), lambda i: (i, 0)),
            pl.BlockSpec((tm, mix_a.shape[1]), lambda i: (i, 0)),
            pl.BlockSpec((tm, mix_b.shape[1]), lambda i: (i, 0)),
            pl.BlockSpec(g.shape, lambda i: (0, 0)),
            pl.BlockSpec(wo.shape, lambda i: (0, 0)),
        ],
        out_specs=pl.BlockSpec((tm, d), lambda i: (i, 0)),
        out_shape=jax.ShapeDtypeStruct((m, d), F32),
        compiler_params=_cparams(("parallel",)),
        name="mixout",
    )(x, mix_a, mix_b, g, wo)


def _ple_kernel(x_ref, p_ref, g_ref, wpp_ref, wpg_ref, o_ref):
    x = x_ref[...]
    gate = jax.nn.sigmoid(_dot(_rms(x, g_ref[NG_PLE_PRE:NG_PLE_PRE + 1, :]).astype(BF16), wpg_ref[...]))
    proj = _dot(p_ref[...].astype(BF16), wpp_ref[...])
    o_ref[...] = x + _rms(proj * gate, g_ref[NG_PLE_POST:NG_PLE_POST + 1, :])


def _ple(x, p, g, wpp, wpg):
    m, d = x.shape
    tm = _pick_tile(m, 512)
    return pl.pallas_call(
        _ple_kernel,
        grid=(m // tm,),
        in_specs=[
            pl.BlockSpec((tm, d), lambda i: (i, 0)),
            pl.BlockSpec((tm, p.shape[1]), lambda i: (i, 0)),
            pl.BlockSpec(g.shape, lambda i: (0, 0)),
            pl.BlockSpec(wpp.shape, lambda i: (0, 0)),
            pl.BlockSpec(wpg.shape, lambda i: (0, 0)),
        ],
        out_specs=pl.BlockSpec((tm, d), lambda i: (i, 0)),
        out_shape=jax.ShapeDtypeStruct((m, d), F32),
        compiler_params=_cparams(("parallel",)),
        name="ple",
    )(x, p, g, wpp, wpg)


def _pair_masks(q):
    lane = _iota(q.shape, 1)
    return (jnp.where(lane < HEAD_DIM, q, 0.0).astype(BF16), jnp.where(lane >= HEAD_DIM, q, 0.0).astype(BF16))


def _sb_prompt_kernel(q_ref, k_ref, v_ref, o_ref, acc_scr, c_scr, *, tq):
    i = pl.program_id(2)
    qm = _pair_masks(q_ref[...] * HEAD_DIM ** -0.5)
    row = _iota((tq, tq), 0)
    col = _iota((tq, tq), 1)
    upper = jnp.where(row > col, 1.0, 0.0).astype(BF16)
    acc_scr[...] = jnp.zeros_like(acc_scr)
    c_scr[...] = jnp.zeros_like(c_scr)

    def block(kb, diag):
        ks = pl.multiple_of(kb * tq, tq)
        kblk = k_ref[pl.ds(ks, tq), :].astype(BF16)
        vblk = v_ref[pl.ds(ks, tq), :].astype(BF16)
        for h in range(2):
            z = _dot_nt(qm[h], kblk)
            lg = _log_sigmoid(-z)
            if diag:
                valid = col < row
                lg = jnp.where(valid, lg, 0.0)
            hi, lo = _split2(lg)
            aft = _dot(hi, upper) + _dot(lo, upper)
            c = c_scr[h]
            w = jnp.exp(z + lg + aft + c)
            if diag:
                w = jnp.where(valid, w, 0.0)
            acc_scr[h] += _dot(w.astype(BF16), vblk)
            c_scr[h] = c + aft[:, 0:1] + lg[:, 0:1]

    block(i, True)

    def body(j, carry):
        block(i - 1 - j, False)
        return carry

    lax.fori_loop(0, i, body, 0)
    lane = _iota((tq, LANES), 1)
    o_ref[...] = jnp.where(lane < HEAD_DIM, acc_scr[0], acc_scr[1])


def _sb_prompt(q, kv, nb, t):
    tq = _pick_tile(t, 256)
    nq = t // tq
    npair = q.shape[1] // LANES
    return pl.pallas_call(
        functools.partial(_sb_prompt_kernel, tq=tq),
        grid=(nb, npair, nq),
        in_specs=[
            pl.BlockSpec((tq, LANES), lambda b, p, i: (b * nq + i, p)),
            pl.BlockSpec((t, LANES), lambda b, p, i: (b, p)),
            pl.BlockSpec((t, LANES), lambda b, p, i: (b, npair + p)),
        ],
        out_specs=pl.BlockSpec((tq, LANES), lambda b, p, i: (b * nq + i, p)),
        out_shape=jax.ShapeDtypeStruct(q.shape, F32),
        scratch_shapes=[pltpu.VMEM((2, tq, LANES), F32), pltpu.VMEM((2, tq, 1), F32)],
        compiler_params=_cparams(("parallel", "parallel", "arbitrary")),
        name="sb_prompt",
    )(q, kv, kv)


def _cumsum_kernel(x_ref, o_ref, carry_scr, *, tc):
    @pl.when(pl.program_id(1) == 0)
    def _():
        carry_scr[...] = jnp.zeros_like(carry_scr)

    row = _iota((tc, tc), 0)
    col = _iota((tc, tc), 1)
    lower = jnp.where(row >= col, 1.0, 0.0).astype(BF16)
    hi, mid, lo = _split3(x_ref[...])
    cs = _dot(lower, hi) + _dot(lower, mid) + _dot(lower, lo) + carry_scr[...]
    o_ref[...] = cs
    carry_scr[...] = cs[tc - 1:tc, :]


def _cumsum_rows(x, nb, t):
    tc = _pick_tile(t, 512)
    nt = t // tc
    return pl.pallas_call(
        functools.partial(_cumsum_kernel, tc=tc),
        grid=(nb, nt),
        in_specs=[pl.BlockSpec((tc, LANES), lambda b, j: (b * nt + j, 0))],
        out_specs=pl.BlockSpec((tc, LANES), lambda b, j: (b * nt + j, 0)),
        out_shape=jax.ShapeDtypeStruct(x.shape, F32),
        scratch_shapes=[pltpu.VMEM((1, LANES), F32)],
        compiler_params=_cparams(("parallel", "arbitrary")),
        name="cumsum_rows",
    )(x)


def _fox_prompt_kernel(q_ref, k_ref, v_ref, cq_ref, ck_ref, o_ref, acc_scr, m_scr, l_scr, *, tq):
    p = pl.program_id(1)
    i = pl.program_id(2)
    qm = _pair_masks(q_ref[...] * HEAD_DIM ** -0.5)
    cq_all = cq_ref[...]
    lane_c = _iota(cq_all.shape, 1)
    cq = [jnp.sum(jnp.where(lane_c == 2 * p + h, cq_all, 0.0), axis=1, keepdims=True) for h in range(2)]
    row = _iota((tq, tq), 0)
    col = _iota((tq, tq), 1)
    acc_scr[...] = jnp.zeros_like(acc_scr)
    m_scr[...] = jnp.full_like(m_scr, NEG_BIG)
    l_scr[...] = jnp.zeros_like(l_scr)

    def block(kb, diag):
        ks = pl.multiple_of(kb * tq, tq)
        kblk = k_ref[pl.ds(ks, tq), :].astype(BF16)
        vblk = v_ref[pl.ds(ks, tq), :].astype(BF16)
        for h in range(2):
            s = _dot_nt(qm[h], kblk) + (cq[h] - ck_ref[0, 0, h:h + 1, pl.ds(ks, tq)])
            if diag:
                s = jnp.where(col <= row, s, NEG_BIG)
            m_old = m_scr[h]
            m_new = jnp.maximum(m_old, jnp.max(s, axis=1, keepdims=True))
            alpha = jnp.exp(m_old - m_new)
            pr = jnp.exp(s - m_new)
            l_scr[h] = alpha * l_scr[h] + jnp.sum(pr, axis=1, keepdims=True)
            acc_scr[h] = alpha * acc_scr[h] + _dot(pr.astype(BF16), vblk)
            m_scr[h] = m_new

    block(i, True)

    def body(j, carry):
        block(j, False)
        return carry

    lax.fori_loop(0, i, body, 0)
    lane = _iota((tq, LANES), 1)
    o_ref[...] = jnp.where(lane < HEAD_DIM, acc_scr[0] / l_scr[0], acc_scr[1] / l_scr[1])


def _fox_prompt(q, kv, cum, cum_t, nb, t):
    tq = _pick_tile(t, 256)
    nq = t // tq
    npair = q.shape[1] // LANES
    return pl.pallas_call(
        functools.partial(_fox_prompt_kernel, tq=tq),
        grid=(nb, npair, nq),
        in_specs=[
            pl.BlockSpec((tq, LANES), lambda b, p, i: (b * nq + i, p)),
            pl.BlockSpec((t, LANES), lambda b, p, i: (b, p)),
            pl.BlockSpec((t, LANES), lambda b, p, i: (b, npair + p)),
            pl.BlockSpec((tq, LANES), lambda b, p, i: (b * nq + i, 0)),
            pl.BlockSpec((1, 1, 2, t), lambda b, p, i: (b, p, 0, 0)),
        ],
        out_specs=pl.BlockSpec((tq, LANES), lambda b, p, i: (b * nq + i, p)),
        out_shape=jax.ShapeDtypeStruct(q.shape, F32),
        scratch_shapes=[pltpu.VMEM((2, tq, LANES), F32), pltpu.VMEM((2, tq, 1), F32), pltpu.VMEM((2, tq, 1), F32)],
        compiler_params=_cparams(("parallel", "parallel", "arbitrary")),
        name="fox_prompt",
    )(q, kv, kv, cum, cum_t)


def _gla_kernel(qkv_ref, la_ref, r_ref, s0_ref, gn_ref, o_ref, sfin_ref, st_scr, o_scr, e_scr, *, chunk, tt):
    j = pl.program_id(1)
    nk = H_GLA * DK_GLA
    nv = H_GLA * DV_GLA
    c = chunk

    @pl.when(j == 0)
    def _():
        st_scr[...] = s0_ref[0]

    expand = jnp.where(_iota((nk, nv), 0) // DK_GLA == _iota((nk, nv), 1) // DV_GLA, 1.0, 0.0).astype(BF16)
    diag_t = jnp.where(_iota((nv, nk), 0) // DV_GLA == _iota((nv, nk), 1) // DK_GLA, 1.0, 0.0)
    rowc = _iota((c, nk), 0)

    def chunk_body(ci, carry):
        r0 = pl.multiple_of(ci * c, c)
        q = qkv_ref[0, pl.ds(r0, c), 0:nk] * DK_GLA ** -0.5
        k = qkv_ref[0, pl.ds(r0, c), nk:2 * nk]
        v = qkv_ref[0, pl.ds(r0, c), 2 * nk:2 * nk + nv]
        g = la_ref[0, pl.ds(r0, c), :]
        b = jnp.zeros((c, nk), F32)
        for s in range(c):
            b = b + jnp.where(rowc >= s, g[s:s + 1, :], 0.0)
        for s in range(c):
            decay = jnp.where(rowc >= s, jnp.exp(b - b[s:s + 1, :]), 0.0)
            e_scr[s * c:(s + 1) * c, :] = decay * q * k[s:s + 1, :]
        hi, lo = _split2(e_scr[...])
        att = _dot(hi, expand) + _dot(lo, expand)
        st = st_scr[...]
        o = _dot_nt((q * jnp.exp(b)).astype(BF16), st.astype(BF16))
        for s in range(c):
            o = o + att[s * c:(s + 1) * c, :] * v[s:s + 1, :]
        o_scr[pl.ds(r0, c), :] = o
        b_last = b[c - 1:c, :]
        kd = k * jnp.exp(b_last - b)
        st_scr[...] = st * jnp.exp(b_last) + diag_t * _dot_tn(v.astype(BF16), kd.astype(BF16))
        return carry

    lax.fori_loop(0, tt // c, chunk_body, 0)

    o = o_scr[...]
    r = r_ref[0]
    gn = gn_ref[...]
    for h in range(H_GLA):
        sl = slice(h * DV_GLA, (h + 1) * DV_GLA)
        oh = o[:, sl]
        rh = r[:, sl]
        y = oh * lax.rsqrt(jnp.mean(oh * oh, axis=-1, keepdims=True) + EPS) * gn
        o_ref[0, :, sl] = y * (rh * jax.nn.sigmoid(rh))

    @pl.when(j == pl.num_programs(1) - 1)
    def _():
        sfin_ref[0] = st_scr[...]


def _gla(qkv, la, r, s0_t, gnorm):
    nb, t, _ = qkv.shape
    chunk = min(GLA_CHUNK, t)
    tt = _pick_tile(t, 512)
    nk, nv = H_GLA * DK_GLA, H_GLA * DV_GLA
    return pl.pallas_call(
        functools.partial(_gla_kernel, chunk=chunk, tt=tt),
        grid=(nb, t // tt),
        in_specs=[
            pl.BlockSpec((1, tt, qkv.shape[2]), lambda b, j: (b, j, 0)),
            pl.BlockSpec((1, tt, nk), lambda b, j: (b, j, 0)),
            pl.BlockSpec((1, tt, nv), lambda b, j: (b, j, 0)),
            pl.BlockSpec((1, nv, nk), lambda b, j: (b, 0, 0)),
            pl.BlockSpec(gnorm.shape, lambda b, j: (0, 0)),
        ],
        out_specs=[
            pl.BlockSpec((1, tt, nv), lambda b, j: (b, j, 0)),
            pl.BlockSpec((1, nv, nk), lambda b, j: (b, 0, 0)),
        ],
        out_shape=[jax.ShapeDtypeStruct((nb, t, nv), F32), jax.ShapeDtypeStruct(s0_t.shape, F32)],
        scratch_shapes=[pltpu.VMEM((nv, nk), F32), pltpu.VMEM((tt, nv), F32), pltpu.VMEM((chunk * chunk, nk), F32)],
        compiler_params=_cparams(("parallel", "arbitrary")),
        name="gla",
    )(qkv, la, r, s0_t, gnorm)


def _t5_bucket(dist):
    max_exact = N_BUCKETS // 2
    d = jnp.maximum(dist, 0)
    large = max_exact + (jnp.log(jnp.maximum(d, 1).astype(F32) / max_exact)
                         / math.log(MAX_DISTANCE / max_exact) * (N_BUCKETS - max_exact)).astype(I32)
    return jnp.where(d < max_exact, d, jnp.minimum(large, N_BUCKETS - 1))


def _bias_tables_kernel(rb_ref, bp_ref, bs_ref, *, tq, past):
    bucket_p = _t5_bucket(_iota((tq, 2 * tq), 0) - _iota((tq, 2 * tq), 1) + tq)
    rows = bs_ref.shape[0]
    bucket_s = _t5_bucket(past + _iota(bs_ref.shape, 0) // H_DSA - _iota(bs_ref.shape, 1))
    head_s = _iota(bs_ref.shape, 0) % H_DSA
    acc_s = jnp.zeros(bs_ref.shape, F32)
    for h in range(H_DSA):
        acc_p = jnp.zeros((tq, 2 * tq), F32)
        for bkt in range(N_BUCKETS):
            val = rb_ref[bkt, h]
            acc_p = jnp.where(bucket_p == bkt, val, acc_p)
            acc_s = jnp.where((bucket_s == bkt) & (head_s == h), val, acc_s)
        bp_ref[h] = acc_p
    del rows
    bs_ref[...] = acc_s


def _bias_tables(rel_bias, tq, past, n_new):
    cols_s = past + LANES
    return pl.pallas_call(
        functools.partial(_bias_tables_kernel, tq=tq, past=past),
        in_specs=[pl.BlockSpec(memory_space=pltpu.SMEM)],
        out_specs=[pl.BlockSpec(memory_space=pltpu.VMEM), pl.BlockSpec(memory_space=pltpu.VMEM)],
        out_shape=[jax.ShapeDtypeStruct((H_DSA, tq, 2 * tq), F32),
                   jax.ShapeDtypeStruct((n_new * H_DSA, cols_s), F32)],
        compiler_params=pltpu.CompilerParams(vmem_limit_bytes=VMEM_LIMIT_BYTES),
        name="bias_tables",
    )(rel_bias)


def _sort_key(score):
    bits = lax.bitcast_convert_type(score + 0.0, I32)
    return bits ^ ((bits >> 31) & 0x7FFFFFFF)


def _select_threshold(key_scr, n_chunks, topk, rows):
    def count(pred):
        def body(cb, acc):
            kk = key_scr[:, pl.ds(pl.multiple_of(cb * LANES, LANES), LANES)]
            idx = cb * LANES + _iota((rows, LANES), 1)
            return acc + jnp.where(pred(kk, idx), 1.0, 0.0)

        acc = lax.fori_loop(0, n_chunks, body, jnp.zeros((rows, LANES), F32))
        return jnp.sum(acc, axis=1, keepdims=True)

    kf = float(topk)
    thr0 = jnp.where(count(lambda kk, idx: kk >= 0) >= kf, 0, INT_MIN).astype(I32)

    def bit_body(t, thr):
        cand = thr + (jnp.int32(1) << (30 - t))
        return jnp.where(count(lambda kk, idx: kk >= cand) >= kf, cand, thr)

    thr = lax.fori_loop(0, 31, bit_body, thr0)
    n_gt = count(lambda kk, idx: kk > thr)
    n_eq = count(lambda kk, idx: kk == thr)
    need = kf - n_gt
    trim = (n_eq > need) & (thr > INT_MIN)

    @pl.when(jnp.max(jnp.where(trim, 1.0, 0.0)) > 0.0)
    def _():
        n_bits = max(1, int(math.ceil(math.log2(key_scr.shape[1] + 1))))

        def idx_body(t, cut):
            cand = cut + (jnp.int32(1) << (n_bits - 1 - t))
            return jnp.where(count(lambda kk, idx: (kk == thr) & (idx < cand)) < need, cand, cut)

        cut = lax.fori_loop(0, n_bits, idx_body, jnp.zeros((rows, 1), I32))

        def demote(cb, carry):
            sl = pl.ds(pl.multiple_of(cb * LANES, LANES), LANES)
            kk = key_scr[:, sl]
            idx = cb * LANES + _iota((rows, LANES), 1)
            key_scr[:, sl] = jnp.where(trim & (kk == thr) & (idx > cut), kk - 1, kk)
            return carry

        lax.fori_loop(0, n_chunks, demote, 0)

    return jnp.maximum(thr, INT_MIN + 1)


def _dsa_prompt_kernel(rb_ref, q_ref, iq_ref, mq_ref, kv_ref, ik_ref, bias_ref, o_ref,
                       key_scr, acc_scr, m_scr, l_scr, *, tq, topk):
    i = pl.program_id(1)
    npair = H_DSA // 2
    row = _iota((tq, tq), 0)
    col = _iota((tq, tq), 1)

    iq = iq_ref[...]
    mq = mq_ref[...]
    iqm = []
    for p in range(H_IDX // 2):
        iqm.extend(_pair_masks(iq[:, p * LANES:(p + 1) * LANES]))
    wq = [mq[:, MISC_IW + h:MISC_IW + h + 1] * H_IDX ** -0.5 for h in range(H_IDX)]

    def score_block(kb, diag):
        ks = pl.multiple_of(kb * tq, tq)
        ik2 = ik_ref[pl.ds(ks, tq), :].astype(BF16)
        score = jnp.zeros((tq, tq), F32)
        for h in range(H_IDX):
            score = score + jnp.maximum(_dot_nt(iqm[h], ik2), 0.0) * wq[h]
        key = _sort_key(score)
        if diag:
            key = jnp.where(col <= row, key, INT_MIN)
        key_scr[:, pl.ds(ks, tq)] = key

    score_block(i, True)

    def score_body(kb, carry):
        score_block(kb, False)
        return carry

    lax.fori_loop(0, i, score_body, 0)

    thr = _select_threshold(key_scr, (i + 1) * (tq // LANES), topk, tq)

    q = q_ref[...] * HEAD_DIM ** -0.5
    qm = []
    for p in range(npair):
        qm.extend(_pair_masks(q[:, p * LANES:(p + 1) * LANES]))
    acc_scr[...] = jnp.zeros_like(acc_scr)
    m_scr[...] = jnp.full_like(m_scr, NEG_BIG)
    l_scr[...] = jnp.zeros_like(l_scr)

    def attend_block(kb, near):
        ks = pl.multiple_of(kb * tq, tq)
        sel = key_scr[:, pl.ds(ks, tq)] >= thr
        for p in range(npair):
            kblk = kv_ref[pl.ds(ks, tq), p * LANES:(p + 1) * LANES]
            vblk = kv_ref[pl.ds(ks, tq), (npair + p) * LANES:(npair + p + 1) * LANES]
            for hh in range(2):
                h = 2 * p + hh
                if near == 0:
                    bias = rb_ref[N_BUCKETS - 1, h]
                else:
                    bias = bias_ref[h, :, (near - 1) * tq:near * tq]
                s = jnp.where(sel, _dot_nt(qm[h], kblk) + bias, NEG_BIG)
                m_old = m_scr[h]
                m_new = jnp.maximum(m_old, jnp.max(s, axis=1, keepdims=True))
                alpha = jnp.exp(m_old - m_new)
                pr = jnp.exp(s - m_new)
                l_scr[h] = alpha * l_scr[h] + jnp.sum(pr, axis=1, keepdims=True)
                acc_scr[h] = alpha * acc_scr[h] + _dot(pr.astype(BF16), vblk)
                m_scr[h] = m_new

    attend_block(i, 2)

    @pl.when(i >= 1)
    def _():
        attend_block(i - 1, 1)

    def attend_body(kb, carry):
        attend_block(kb, 0)
        return carry

    lax.fori_loop(0, jnp.maximum(i - 1, 0), attend_body, 0)
    lane = _iota((tq, LANES), 1)
    for p in range(npair):
        h0, h1 = 2 * p, 2 * p + 1
        o_ref[:, p * LANES:(p + 1) * LANES] = jnp.where(lane < HEAD_DIM, acc_scr[h0] / l_scr[h0],
                                                           acc_scr[h1] / l_scr[h1])


def _dsa_prompt(rel_bias, q, iq, misc, kv_bf16, ik2, bias_p, nb, t, tq):
    nq = t // tq
    topk = min(DSA_TOPK, t // 4)
    single = pl.Buffered(1)
    return pl.pallas_call(
        functools.partial(_dsa_prompt_kernel, tq=tq, topk=topk),
        grid=(nb, nq),
        in_specs=[
            pl.BlockSpec(memory_space=pltpu.SMEM),
            pl.BlockSpec((tq, q.shape[1]), lambda b, i: (b * nq + i, 0)),
            pl.BlockSpec((tq, iq.shape[1]), lambda b, i: (b * nq + i, 0)),
            pl.BlockSpec((tq, LANES), lambda b, i: (b * nq + i, 0)),
            pl.BlockSpec((t, kv_bf16.shape[1]), lambda b, i: (b, 0), pipeline_mode=single),
            pl.BlockSpec((t, LANES), lambda b, i: (b, 0), pipeline_mode=single),
            pl.BlockSpec(bias_p.shape, lambda b, i: (0, 0, 0), pipeline_mode=single),
        ],
        out_specs=pl.BlockSpec((tq, q.shape[1]), lambda b, i: (b * nq + i, 0)),
        out_shape=jax.ShapeDtypeStruct(q.shape, F32),
        scratch_shapes=[pltpu.VMEM((tq, t), I32), pltpu.VMEM((H_DSA, tq, LANES), F32),
                        pltpu.VMEM((H_DSA, tq, 1), F32), pltpu.VMEM((H_DSA, tq, 1), F32)],
        compiler_params=_cparams(("parallel", "arbitrary")),
        name="dsa_prompt",
    )(rel_bias, q, iq, misc, kv_bf16, ik2, bias_p)


def _rows_by_head(x, n_heads):
    t, w = x.shape
    head_mask = _iota((n_heads, w), 1) // HEAD_DIM == _iota((n_heads, w), 0)
    return jnp.concatenate([jnp.where(head_mask, x[i:i + 1, :], 0.0) for i in range(t)], axis=0)


def _collapse_heads(acc, n_new, n_heads):
    w = acc.shape[1]
    head_mask = _iota((n_heads, w), 1) // HEAD_DIM == _iota((n_heads, w), 0)
    rows = [jnp.sum(jnp.where(head_mask, acc[i * n_heads:(i + 1) * n_heads, :], 0.0), axis=0, keepdims=True)
            for i in range(n_new)]
    return jnp.concatenate(rows, axis=0)


def _sb_sample_kernel(pt_ref, q_ref, kvn_ref, *refs, n_pages, n_new):
    del pt_ref
    pages = refs[:n_pages]
    o_ref = refs[n_pages]
    hw = H_SB * HEAD_DIM
    qbd = _rows_by_head(q_ref[0] * HEAD_DIM ** -0.5, H_SB)
    qbd_b = qbd.astype(BF16)
    nr = n_new * H_SB
    trow = _iota((nr, 1), 0) // H_SB
    kn = kvn_ref[0, :, :hw]
    vn = kvn_ref[0, :, hw:]
    lg_new, z_new = [], []
    for j in range(n_new):
        z = jnp.sum(qbd * kn[j:j + 1, :], axis=1, keepdims=True)
        z_new.append(z)
        lg_new.append(jnp.where(j < trow, _log_sigmoid(-z), 0.0))
    acc = jnp.zeros((nr, hw), F32)
    carry = jnp.zeros((nr, 1), F32)
    for j in reversed(range(n_new)):
        w = jnp.where(j < trow, jnp.exp(z_new[j] + lg_new[j] + carry), 0.0)
        acc = acc + w * vn[j:j + 1, :]
        carry = carry + lg_new[j]
    upper = jnp.where(_iota((LANES, LANES), 0) > _iota((LANES, LANES), 1), 1.0, 0.0).astype(BF16)
    for pg in reversed(range(n_pages)):
        kp = pages[pg][0, 0, :, :hw].astype(BF16)
        vp = pages[pg][0, 0, :, hw:].astype(BF16)
        z = _dot_nt(qbd_b, kp)
        lg = _log_sigmoid(-z)
        hi, lo = _split2(lg)
        aft = _dot(hi, upper) + _dot(lo, upper)
        w = jnp.exp(z + lg + aft + carry)
        acc = acc + _dot(w.astype(BF16), vp)
        carry = carry + aft[:, 0:1] + lg[:, 0:1]
    o_ref[0] = _collapse_heads(acc, n_new, H_SB)


def _page_specs(n_pages, layer, block):
    def spec(j):
        return pl.BlockSpec(block, lambda b, pt: (layer, pt[b, j]) + (0,) * (len(block) - 2))
    return [spec(j) for j in range(n_pages)]


def _sb_sample(page_table, q, kv_new, cache, layer):
    db, n_new, hw = q.shape
    n_pages = page_table.shape[1]
    page = cache.shape[2]
    grid_spec = pltpu.PrefetchScalarGridSpec(
        num_scalar_prefetch=1,
        grid=(db,),
        in_specs=[pl.BlockSpec((1, n_new, hw), lambda b, pt: (b, 0, 0)),
                  pl.BlockSpec((1, n_new, 2 * hw), lambda b, pt: (b, 0, 0))]
        + _page_specs(n_pages, layer, (1, 1, page, 2 * hw)),
        out_specs=pl.BlockSpec((1, n_new, hw), lambda b, pt: (b, 0, 0)),
    )
    return pl.pallas_call(
        functools.partial(_sb_sample_kernel, n_pages=n_pages, n_new=n_new),
        grid_spec=grid_spec,
        out_shape=jax.ShapeDtypeStruct(q.shape, F32),
        compiler_params=_cparams(("parallel",)),
        name="sb_sample",
    )(page_table, q, kv_new, *([cache] * n_pages))


def _head_column(x, n_new, n_heads, pick):
    lane_is_head = _iota((n_heads, x.shape[1]), 1) == _iota((n_heads, x.shape[1]), 0)
    cols = [jnp.sum(jnp.where(lane_is_head, x[pick(i):pick(i) + 1, :], 0.0), axis=1, keepdims=True)
            for i in range(n_new)]
    return jnp.concatenate(cols, axis=0)


def _fox_sample_kernel(pt_ref, q_ref, kvn_ref, lfn_ref, *refs, n_pages, n_new):
    del pt_ref
    kv_pages = refs[:n_pages]
    lf_pages = refs[n_pages:2 * n_pages]
    o_ref = refs[2 * n_pages]
    hw = H_FOX * HEAD_DIM
    qbd = _rows_by_head(q_ref[0] * HEAD_DIM ** -0.5, H_FOX)
    qbd_b = qbd.astype(BF16)
    nr = n_new * H_FOX
    trow = _iota((nr, 1), 0) // H_FOX
    kn = kvn_ref[0, :, :hw]
    vn = kvn_ref[0, :, hw:]
    lfn = lfn_ref[0]
    csum = [lfn[0:1, :]]
    for j in range(1, n_new):
        csum.append(csum[-1] + lfn[j:j + 1, :])
    csum = jnp.concatenate(csum, axis=0)
    c_query = _head_column(csum, n_new, H_FOX, lambda t: t)
    s_new = []
    for j in range(n_new):
        c_key = _head_column(csum, n_new, H_FOX, lambda t, j=j: j)
        z = jnp.sum(qbd * kn[j:j + 1, :], axis=1, keepdims=True)
        s_new.append(jnp.where(j <= trow, z + (c_query - c_key), NEG_BIG))
    m = s_new[0]
    for j in range(1, n_new):
        m = jnp.maximum(m, s_new[j])
    l = jnp.zeros((nr, 1), F32)
    acc = jnp.zeros((nr, hw), F32)
    for j in range(n_new):
        pr = jnp.exp(s_new[j] - m)
        l = l + pr
        acc = acc + pr * vn[j:j + 1, :]
    upper = jnp.where(_iota((LANES, LANES), 0) > _iota((LANES, LANES), 1), 1.0, 0.0).astype(BF16)
    carry = c_query
    for pg in reversed(range(n_pages)):
        kp = kv_pages[pg][0, 0, :, :hw].astype(BF16)
        vp = kv_pages[pg][0, 0, :, hw:].astype(BF16)
        lf = jnp.concatenate([lf_pages[pg][0, 0]] * n_new, axis=0)
        hi, mid, lo = _split3(lf)
        aft = _dot(hi, upper) + _dot(mid, upper) + _dot(lo, upper)
        s = _dot_nt(qbd_b, kp) + aft + carry
        m_new = jnp.maximum(m, jnp.max(s, axis=1, keepdims=True))
        alpha = jnp.exp(m - m_new)
        pr = jnp.exp(s - m_new)
        l = alpha * l + jnp.sum(pr, axis=1, keepdims=True)
        acc = alpha * acc + _dot(pr.astype(BF16), vp)
        m = m_new
        carry = carry + aft[:, 0:1] + lf[:, 0:1]
    o_ref[0] = _collapse_heads(acc / l, n_new, H_FOX)


def _fox_sample(page_table, q, kv_new, lf_new, cache_kv, cache_lf_t, layer):
    db, n_new, hw = q.shape
    n_pages = page_table.shape[1]
    page = cache_kv.shape[2]
    grid_spec = pltpu.PrefetchScalarGridSpec(
        num_scalar_prefetch=1,
        grid=(db,),
        in_specs=[pl.BlockSpec((1, n_new, hw), lambda b, pt: (b, 0, 0)),
                  pl.BlockSpec((1, n_new, 2 * hw), lambda b, pt: (b, 0, 0)),
                  pl.BlockSpec((1, n_new, LANES), lambda b, pt: (b, 0, 0))]
        + _page_specs(n_pages, layer, (1, 1, page, 2 * hw))
        + _page_specs(n_pages, layer, (1, 1, H_FOX, page)),
        out_specs=pl.BlockSpec((1, n_new, hw), lambda b, pt: (b, 0, 0)),
    )
    return pl.pallas_call(
        functools.partial(_fox_sample_kernel, n_pages=n_pages, n_new=n_new),
        grid_spec=grid_spec,
        out_shape=jax.ShapeDtypeStruct(q.shape, F32),
        compiler_params=_cparams(("parallel",)),
        name="fox_sample",
    )(page_table, q, kv_new, lf_new, *([cache_kv] * n_pages), *([cache_lf_t] * n_pages))


def _dsa_sample_kernel(pt_ref, q_ref, kvn_ref, iq_ref, iw_ref, ikn_ref, bias_ref, *refs, n_pages, n_new, topk):
    del pt_ref
    kv_pages = refs[:n_pages]
    ik_pages = refs[n_pages:2 * n_pages]
    o_ref = refs[2 * n_pages]
    key_scr = refs[2 * n_pages + 1]
    hw = H_DSA * HEAD_DIM
    nr = n_new * H_DSA
    page = kv_pages[0].shape[2]
    past = n_pages * page
    trow = _iota((nr, LANES), 0) // H_DSA
    lane = _iota((nr, LANES), 1)

    iq = iq_ref[0].astype(BF16)
    iw = iw_ref[0] * H_IDX ** -0.5

    def head_sum(x):
        cols = x.shape[1]
        g = jnp.sum(x.reshape(n_new, H_IDX, cols), axis=1, keepdims=True)
        return jnp.broadcast_to(g, (n_new, H_IDX, cols)).reshape(nr, cols)

    for pg in range(n_pages):
        ikp = ik_pages[pg][0, 0].astype(BF16)
        sc = head_sum(jnp.maximum(_dot_nt(iq, ikp), 0.0) * iw)
        key_scr[:, pg * page:(pg + 1) * page] = _sort_key(sc)
    ikn = ikn_ref[0].astype(BF16)
    sc = head_sum(jnp.maximum(_dot_nt(iq, ikn), 0.0) * iw)
    key_scr[:, past:past + LANES] = jnp.where(lane <= trow, _sort_key(sc), INT_MIN)

    thr = _select_threshold(key_scr, (past + LANES) // LANES, topk, nr)

    qbd_b = _rows_by_head(q_ref[0] * HEAD_DIM ** -0.5, H_DSA).astype(BF16)
    m = jnp.full((nr, 1), NEG_BIG, F32)
    l = jnp.zeros((nr, 1), F32)
    acc = jnp.zeros((nr, hw), F32)
    for pg in range(n_pages + 1):
        if pg < n_pages:
            kp = kv_pages[pg][0, 0, :, :hw].astype(BF16)
            vp = kv_pages[pg][0, 0, :, hw:].astype(BF16)
        else:
            kp = kvn_ref[0, :, :hw].astype(BF16)
            vp = kvn_ref[0, :, hw:].astype(BF16)
        sel = key_scr[:, pg * page:(pg + 1) * page] >= thr
        s = jnp.where(sel, _dot_nt(qbd_b, kp) + bias_ref[:, pg * page:(pg + 1) * page], NEG_BIG)
        m_new = jnp.maximum(m, jnp.max(s, axis=1, keepdims=True))
        alpha = jnp.exp(m - m_new)
        pr = jnp.exp(s - m_new)
        l = alpha * l + jnp.sum(pr, axis=1, keepdims=True)
        acc = alpha * acc + _dot(pr.astype(BF16), vp)
        m = m_new
    o_ref[0] = _collapse_heads(acc / l, n_new, H_DSA)


def _dsa_sample(page_table, q, kv_new_pad, iq_rows, iw_rows, ik_new_pad, bias_s, cache_kv, cache_ik, layer):
    db, n_new, hw = q.shape
    n_pages = page_table.shape[1]
    page = cache_kv.shape[2]
    nr = n_new * H_DSA
    topk = min(DSA_TOPK, (n_pages * page + n_new) // 4)
    grid_spec = pltpu.PrefetchScalarGridSpec(
        num_scalar_prefetch=1,
        grid=(db,),
        in_specs=[pl.BlockSpec((1, n_new, hw), lambda b, pt: (b, 0, 0)),
                  pl.BlockSpec((1, page, 2 * hw), lambda b, pt: (b, 0, 0)),
                  pl.BlockSpec((1, nr, D_IDX), lambda b, pt: (b, 0, 0)),
                  pl.BlockSpec((1, nr, 1), lambda b, pt: (b, 0, 0)),
                  pl.BlockSpec((1, page, D_IDX), lambda b, pt: (b, 0, 0)),
                  pl.BlockSpec(bias_s.shape, lambda b, pt: (0, 0))]
        + _page_specs(n_pages, layer, (1, 1, page, 2 * hw))
        + _page_specs(n_pages, layer, (1, 1, page, D_IDX)),
        out_specs=pl.BlockSpec((1, n_new, hw), lambda b, pt: (b, 0, 0)),
        scratch_shapes=[pltpu.VMEM((nr, n_pages * page + LANES), I32)],
    )
    return pl.pallas_call(
        functools.partial(_dsa_sample_kernel, n_pages=n_pages, n_new=n_new, topk=topk),
        grid_spec=grid_spec,
        out_shape=jax.ShapeDtypeStruct(q.shape, F32),
        compiler_params=_cparams(("parallel",)),
        name="dsa_sample",
    )(page_table, q, kv_new_pad, iq_rows, iw_rows, ik_new_pad, bias_s, *([cache_kv] * n_pages),
      *([cache_ik] * n_pages))


def _state_to_block_diag(s):
    nb = s.shape[0]
    eye = jnp.eye(H_GLA, dtype=s.dtype)
    return jnp.einsum("bhkv,hg->bhvgk", s, eye).reshape(nb, H_GLA * DV_GLA, H_GLA * DK_GLA)


def _block_diag_to_state(bd):
    nb = bd.shape[0]
    bd = bd.reshape(nb, H_GLA, DV_GLA, H_GLA, DK_GLA)
    return jnp.stack([jnp.swapaxes(bd[:, h, :, h, :], 1, 2) for h in range(H_GLA)], axis=1)


def _pad_cols(w, width):
    return jnp.pad(w, ((0, 0), (0, width - w.shape[1])))


def _cd_weight_layout(w):
    gq = H_GLA * DK_GLA
    gv = H_GLA * DV_GLA
    dh = H_DSA * HEAD_DIM
    offs = np.cumsum([0, gq, gq, gv, GLA_RANK, gv, dh, dh, dh, H_IDX * D_IDX, D_IDX, H_IDX])
    seg = [w[:, offs[i]:offs[i + 1]] for i in range(11)]
    q_g, k_g, v_g, g_lr, r_g, q_d, k_d, v_d, iq, ik, iw = seg
    misc = _pad_cols(jnp.concatenate([iw, g_lr], axis=1), LANES)
    return jnp.concatenate([q_g, k_g, v_g, r_g, q_d, k_d, v_d, iq, ik, ik, misc], axis=1)


def kernel(x_prompt, x_sample, p_prompt, p_sample, cache_sb_kv, cache_fox_kv, cache_fox_logf, state_gla,
           cache_dsa_kv, cache_dsa_idxk, page_table, norm_gains, ffn_w_gate, ffn_w_up, ffn_w_down, w_out,
           ple_w_proj, ple_w_gate, ab_w_in, ab_b_forget, cd_w_in, gla_w_gate_up, gla_b_gate, gla_norm, rel_bias):
    nb, t, d = x_prompt.shape
    db, n_new, _ = x_sample.shape
    depth = norm_gains.shape[0]
    bt = nb * t
    n_pages = page_table.shape[1]
    page = cache_sb_kv.shape[2]
    past = n_pages * page
    tq = _pick_tile(t, 256)

    x = jnp.concatenate([x_prompt.reshape(bt, d), x_sample.reshape(db * n_new, d)], axis=0)
    p_all = jnp.concatenate([p_prompt.reshape(depth, bt, -1), p_sample.reshape(depth, db * n_new, -1)], axis=1)

    wg = ffn_w_gate.astype(BF16)
    wu = ffn_w_up.astype(BF16)
    wd = ffn_w_down.astype(BF16)
    wo = w_out.astype(BF16)
    wpp = ple_w_proj.astype(BF16)
    wpg = ple_w_gate.astype(BF16)

    sb_cache = cache_sb_kv.reshape(cache_sb_kv.shape[:3] + (-1,))
    fox_cache = cache_fox_kv.reshape(cache_fox_kv.shape[:3] + (-1,))
    fox_lf_cache_t = jnp.swapaxes(cache_fox_logf, 2, 3)
    dsa_cache = cache_dsa_kv.reshape(cache_dsa_kv.shape[:3] + (-1,))

    bias_p, bias_s = _bias_tables(rel_bias, tq, past, n_new)

    def split(a):
        return a[:bt], a[bt:]

    outs = {k: [] for k in ("sb_p", "sb_s", "fkv_p", "fkv_s", "flf_p", "flf_s", "gs_p", "gs_s", "dkv_p", "dkv_s",
                            "dik_p", "dik_s")}
    for i in range(depth):
        j = i // 2
        g = norm_gains[i]
        x = _ffn(x, g, wg[i, 0], wu[i, 0], wd[i, 0], NG_FFN1_PRE, NG_FFN1_POST)
        if i % 2 == 0:
            w_pad = _pad_cols(ab_w_in[j], AB_WIDTH_PAD).astype(BF16)
            b_pad = _pad_cols(ab_b_forget[j][None, :], LANES)
            q_sb, kv_sb, q_fx, kv_fx, lf = _proj_ab(x, g, w_pad, b_pad)
            q_sb_p, q_sb_s = split(q_sb)
            kv_sb_p, kv_sb_s = split(kv_sb)
            q_fx_p, q_fx_s = split(q_fx)
            kv_fx_p, kv_fx_s = split(kv_fx)
            lf_p, lf_s = split(lf)
            cum = _cumsum_rows(lf_p, nb, t)
            cum_t = jnp.swapaxes(cum[:, :H_FOX].reshape(nb, t, H_FOX), 1, 2).reshape(nb, H_FOX // 2, 2, t)
            o_sb_p = _sb_prompt(q_sb_p, kv_sb_p, nb, t)
            o_fx_p = _fox_prompt(q_fx_p, kv_fx_p, cum, cum_t, nb, t)
            o_sb_s = _sb_sample(page_table, q_sb_s.reshape(db, n_new, -1), kv_sb_s.reshape(db, n_new, -1),
                                sb_cache, j)
            o_fx_s = _fox_sample(page_table, q_fx_s.reshape(db, n_new, -1), kv_fx_s.reshape(db, n_new, -1),
                                 lf_s.reshape(db, n_new, LANES), fox_cache, fox_lf_cache_t, j)
            mix_a = jnp.concatenate([o_sb_p, o_sb_s.reshape(db * n_new, -1)], axis=0)
            mix_b = jnp.concatenate([o_fx_p, o_fx_s.reshape(db * n_new, -1)], axis=0)
            outs["sb_p"].append(kv_sb_p.reshape(nb, t, 2, H_SB, HEAD_DIM))
            outs["sb_s"].append(kv_sb_s.reshape(db, n_new, 2, H_SB, HEAD_DIM))
            outs["fkv_p"].append(kv_fx_p.reshape(nb, t, 2, H_FOX, HEAD_DIM))
            outs["fkv_s"].append(kv_fx_s.reshape(db, n_new, 2, H_FOX, HEAD_DIM))
            outs["flf_p"].append(lf_p[:, :H_FOX].reshape(nb, t, H_FOX))
            outs["flf_s"].append(lf_s[:, :H_FOX].reshape(db, n_new, H_FOX))
        else:
            w_pad = _pad_cols(_cd_weight_layout(cd_w_in[j]), CD_WIDTH_PAD).astype(BF16)
            wgu_pad = jnp.zeros((LANES, H_GLA * DK_GLA), F32).at[MISC_GLR:MISC_GLR + GLA_RANK].set(
                gla_w_gate_up[j]).astype(BF16)
            qkv_g, r_g, q_d, kv_d, iq, ik2, misc, la = _proj_cd(x, g, w_pad, wgu_pad, gla_b_gate[j][None, :])
            gnorm = gla_norm[j][None, :]
            qkv_g_p, qkv_g_s = split(qkv_g)
            la_p, la_s = split(la)
            r_g_p, r_g_s = split(r_g)
            zero_state = jnp.zeros((nb, H_GLA * DV_GLA, H_GLA * DK_GLA), F32)
            o_g_p, s_p = _gla(qkv_g_p.reshape(nb, t, -1), la_p.reshape(nb, t, -1), r_g_p.reshape(nb, t, -1),
                              zero_state, gnorm)
            o_g_s, s_s = _gla(qkv_g_s.reshape(db, n_new, -1), la_s.reshape(db, n_new, -1),
                              r_g_s.reshape(db, n_new, -1), _state_to_block_diag(state_gla[j]), gnorm)
            q_d_p, q_d_s = split(q_d)
            kv_d_p, kv_d_s = split(kv_d)
            iq_p, iq_s = split(iq)
            ik2_p, ik2_s = split(ik2)
            misc_p, misc_s = split(misc)
            o_d_p = _dsa_prompt(rel_bias, q_d_p, iq_p, misc_p, kv_d_p.astype(BF16), ik2_p, bias_p, nb, t, tq)
            pad_rows = ((0, 0), (0, page - n_new), (0, 0))
            kv_new_pad = jnp.pad(kv_d_s.reshape(db, n_new, -1), pad_rows)
            ik_new_pad = jnp.pad(ik2_s[:, :D_IDX].reshape(db, n_new, D_IDX), pad_rows)
            iq_rows = iq_s.reshape(db, n_new * H_IDX, D_IDX)
            iw_rows = misc_s[:, MISC_IW:MISC_IW + H_IDX].reshape(db, n_new * H_IDX, 1)
            o_d_s = _dsa_sample(page_table, q_d_s.reshape(db, n_new, -1), kv_new_pad, iq_rows, iw_rows, ik_new_pad,
                                bias_s, dsa_cache, cache_dsa_idxk, j)
            mix_a = jnp.concatenate([o_g_p.reshape(bt, -1), o_g_s.reshape(db * n_new, -1)], axis=0)
            mix_b = jnp.concatenate([o_d_p, o_d_s.reshape(db * n_new, -1)], axis=0)
            outs["gs_p"].append(_block_diag_to_state(s_p))
            outs["gs_s"].append(_block_diag_to_state(s_s))
            outs["dkv_p"].append(kv_d_p.reshape(nb, t, 2, H_DSA, HEAD_DIM))
            outs["dkv_s"].append(kv_d_s.reshape(db, n_new, 2, H_DSA, HEAD_DIM))
            outs["dik_p"].append(ik2_p[:, :D_IDX].reshape(nb, t, D_IDX))
            outs["dik_s"].append(ik2_s[:, :D_IDX].reshape(db, n_new, D_IDX))
        x = _mixout(x, mix_a, mix_b, g, wo[i])
        x = _ffn(x, g, wg[i, 1], wu[i, 1], wd[i, 1], NG_FFN2_PRE, NG_FFN2_POST)
        x = _ple(x, p_all[i], g, wpp[i], wpg[i])

    y_p, y_s = split(x)
    return (y_p.reshape(nb, t, d), y_s.reshape(db, n_new, d),
            jnp.stack(outs["sb_p"]), jnp.stack(outs["sb_s"]),
            jnp.stack(outs["fkv_p"]), jnp.stack(outs["fkv_s"]),
            jnp.stack(outs["flf_p"]), jnp.stack(outs["flf_s"]),
            jnp.stack(outs["gs_p"]), jnp.stack(outs["gs_s"]),
            jnp.stack(outs["dkv_p"]), jnp.stack(outs["dkv_s"]),
            jnp.stack(outs["dik_p"]), jnp.stack(outs["dik_s"]))
```

```python
import functools
import math

import jax
import jax.numpy as jnp
import numpy as np
from jax import lax
from jax.experimental import pallas as pl
from jax.experimental.pallas import tpu as pltpu

F32 = jnp.float32
BF16 = jnp.bfloat16
I32 = jnp.int32

HEAD_DIM = 64
H_SB = 8
H_FOX = 8
H_GLA = 4
DK_GLA = 64
DV_GLA = 128
GLA_RANK = 16
GLA_TAU = 16.0
GLA_CHUNK = 16
H_DSA = 8
H_IDX = 8
D_IDX = 64
DSA_TOPK = 256
N_BUCKETS = 32
MAX_DISTANCE = 128
EPS = 1e-6
MACARON_WEIGHT = 0.5
NG_FFN1_PRE, NG_FFN1_POST, NG_MIX_PRE, NG_MIX_POST, NG_FFN2_PRE, NG_FFN2_POST, NG_PLE_PRE, NG_PLE_POST = range(8)

LANES = 128
VMEM_LIMIT_BYTES = 56 * 1024 * 1024
NEG_BIG = -1e30
INT_MIN = -2 ** 31


def _cparams(sem):
    return pltpu.CompilerParams(dimension_semantics=sem, vmem_limit_bytes=VMEM_LIMIT_BYTES)


def _pick_tile(n, cap):
    if n <= cap:
        return n
    for t in range(cap - cap % 8, 7, -8):
        if n % t == 0:
            return t
    return n


def _rms(x, g):
    return x * lax.rsqrt(jnp.mean(x * x, axis=-1, keepdims=True) + EPS) * g


def _log_sigmoid(x):
    return jnp.minimum(x, 0.0) - jnp.log1p(jnp.exp(-jnp.abs(x)))


def _dot(a, b):
    return jnp.dot(a, b, preferred_element_type=F32)


def _dot_nt(a, b):
    return lax.dot_general(a, b, (((1,), (1,)), ((), ())), preferred_element_type=F32)


def _dot_tn(a, b):
    return lax.dot_general(a, b, (((0,), (0,)), ((), ())), preferred_element_type=F32)


def _split2(x):
    hi = x.astype(BF16)
    lo = (x - hi.astype(F32)).astype(BF16)
    return hi, lo


def _split3(x):
    hi = x.astype(BF16)
    r = x - hi.astype(F32)
    mid = r.astype(BF16)
    lo = (r - mid.astype(F32)).astype(BF16)
    return hi, mid, lo


def _iota(shape, dim):
    return lax.broadcasted_iota(I32, shape, dim)


def _ffn_kernel(x_ref, g_ref, wg_ref, wu_ref, wd_ref, o_ref, h_scr, acc_scr, *, pre, post):
    f = pl.program_id(1)

    @pl.when(f == 0)
    def _():
        h_scr[...] = _rms(x_ref[...], g_ref[pre:pre + 1, :]).astype(BF16)
        acc_scr[...] = jnp.zeros_like(acc_scr)

    h = h_scr[...]
    gate = _dot(h, wg_ref[...])
    up = _dot(h, wu_ref[...])
    act = gate * jax.nn.sigmoid(gate) * up
    acc_scr[...] += _dot(act.astype(BF16), wd_ref[...])

    @pl.when(f == pl.num_programs(1) - 1)
    def _():
        o_ref[...] = x_ref[...] + MACARON_WEIGHT * _rms(acc_scr[...], g_ref[post:post + 1, :])


def _ffn(x, g, wg, wu, wd, pre, post):
    m, d = x.shape
    ff = wg.shape[1]
    tm = _pick_tile(m, 512)
    tf = ff
    for cand in range(min(ff, 1408), 127, -128):
        if ff % cand == 0:
            tf = cand
            break
    return pl.pallas_call(
        functools.partial(_ffn_kernel, pre=pre, post=post),
        grid=(m // tm, ff // tf),
        in_specs=[
            pl.BlockSpec((tm, d), lambda i, f: (i, 0)),
            pl.BlockSpec(g.shape, lambda i, f: (0, 0)),
            pl.BlockSpec((d, tf), lambda i, f: (0, f)),
            pl.BlockSpec((d, tf), lambda i, f: (0, f)),
            pl.BlockSpec((tf, d), lambda i, f: (f, 0)),
        ],
        out_specs=pl.BlockSpec((tm, d), lambda i, f: (i, 0)),
        out_shape=jax.ShapeDtypeStruct((m, d), F32),
        scratch_shapes=[pltpu.VMEM((tm, d), BF16), pltpu.VMEM((tm, d), F32)],
        compiler_params=_cparams(("parallel", "arbitrary")),
        name="ffn",
    )(x, g, wg, wu, wd)


AB_Q_SB, AB_KV_SB, AB_Q_FX, AB_KV_FX, AB_F = 0, 512, 1536, 2048, 3072
AB_WIDTH_PAD = 3200


def _proj_ab_kernel(x_ref, g_ref, w_ref, b_ref, qsb_ref, kvsb_ref, qfx_ref, kvfx_ref, lf_ref):
    h = _rms(x_ref[...], g_ref[NG_MIX_PRE:NG_MIX_PRE + 1, :]).astype(BF16)
    qsb_ref[...] = _dot(h, w_ref[:, AB_Q_SB:AB_KV_SB])
    kvsb_ref[...] = _dot(h, w_ref[:, AB_KV_SB:AB_Q_FX])
    qfx_ref[...] = _dot(h, w_ref[:, AB_Q_FX:AB_KV_FX])
    kvfx_ref[...] = _dot(h, w_ref[:, AB_KV_FX:AB_F])
    lf_ref[...] = _log_sigmoid(_dot(h, w_ref[:, AB_F:AB_WIDTH_PAD]) + b_ref[...])


def _proj_ab(x, g, w_pad, b_pad):
    m, d = x.shape
    tm = _pick_tile(m, 512)
    widths = (512, 1024, 512, 1024, LANES)
    return pl.pallas_call(
        _proj_ab_kernel,
        grid=(m // tm,),
        in_specs=[
            pl.BlockSpec((tm, d), lambda i: (i, 0)),
            pl.BlockSpec(g.shape, lambda i: (0, 0)),
            pl.BlockSpec(w_pad.shape, lambda i: (0, 0)),
            pl.BlockSpec(b_pad.shape, lambda i: (0, 0)),
        ],
        out_specs=[pl.BlockSpec((tm, w), lambda i: (i, 0)) for w in widths],
        out_shape=[jax.ShapeDtypeStruct((m, w), F32) for w in widths],
        compiler_params=_cparams(("parallel",)),
        name="proj_ab",
    )(x, g, w_pad, b_pad)


CD_QKV_G, CD_R_G, CD_Q_D, CD_KV_D, CD_IQ, CD_IK2, CD_MISC, CD_WIDTH_PAD = 0, 1024, 1536, 2048, 3072, 3584, 3712, 3840
MISC_IW, MISC_GLR = 0, 8


def _proj_cd_kernel(x_ref, g_ref, w_ref, wgu_ref, bg_ref, qkvg_ref, rg_ref, qd_ref, kvd_ref, iq_ref, ik2_ref,
                    misc_ref, la_ref):
    h = _rms(x_ref[...], g_ref[NG_MIX_PRE:NG_MIX_PRE + 1, :]).astype(BF16)
    qkvg_ref[...] = _dot(h, w_ref[:, CD_QKV_G:CD_R_G])
    rg_ref[...] = _dot(h, w_ref[:, CD_R_G:CD_Q_D])
    qd_ref[...] = _dot(h, w_ref[:, CD_Q_D:CD_KV_D])
    kvd_ref[...] = _dot(h, w_ref[:, CD_KV_D:CD_IQ])
    iq_ref[...] = _dot(h, w_ref[:, CD_IQ:CD_IK2])
    ik2_ref[...] = _dot(h, w_ref[:, CD_IK2:CD_MISC])
    misc = _dot(h, w_ref[:, CD_MISC:CD_WIDTH_PAD])
    misc_ref[...] = misc
    la_ref[...] = _log_sigmoid(_dot(misc.astype(BF16), wgu_ref[...]) + bg_ref[...]) / GLA_TAU


def _proj_cd(x, g, w_pad, wgu_pad, bg):
    m, d = x.shape
    tm = _pick_tile(m, 512)
    widths = (1024, 512, 512, 1024, 512, LANES, LANES, H_GLA * DK_GLA)
    return pl.pallas_call(
        _proj_cd_kernel,
        grid=(m // tm,),
        in_specs=[
            pl.BlockSpec((tm, d), lambda i: (i, 0)),
            pl.BlockSpec(g.shape, lambda i: (0, 0)),
            pl.BlockSpec(w_pad.shape, lambda i: (0, 0)),
            pl.BlockSpec(wgu_pad.shape, lambda i: (0, 0)),
            pl.BlockSpec(bg.shape, lambda i: (0, 0)),
        ],
        out_specs=[pl.BlockSpec((tm, w), lambda i: (i, 0)) for w in widths],
        out_shape=[jax.ShapeDtypeStruct((m, w), F32) for w in widths],
        compiler_params=_cparams(("parallel",)),
        name="proj_cd",
    )(x, g, w_pad, wgu_pad, bg)


def _mixout_kernel(x_ref, a_ref, b_ref, g_ref, wo_ref, o_ref):
    wa = a_ref.shape[1]
    y = _dot(a_ref[...].astype(BF16), wo_ref[:wa, :]) + _dot(b_ref[...].astype(BF16), wo_ref[wa:, :])
    o_ref[...] = x_ref[...] + _rms(y, g_ref[NG_MIX_POST:NG_MIX_POST + 1, :])


def _mixout(x, mix_a, mix_b, g, wo):
    m, d = x.shape
    tm = _pick_tile(m, 512)
    return pl.pallas_call(
        _mixout_kernel,
        grid=(m // tm,),
        in_specs=[
            pl.BlockSpec((tm, d), lambda i: (i, 0)),
            pl.BlockSpec((tm, mix_a.shape[1]), lambda i: (i, 0)),
            pl.BlockSpec((tm, mix_b.shape[1]), lambda i: (i, 0)),
            pl.BlockSpec(g.shape, lambda i: (0, 0)),
            pl.BlockSpec(wo.shape, lambda i: (0, 0)),
        ],
        out_specs=pl.BlockSpec((tm, d), lambda i: (i, 0)),
        out_shape=jax.ShapeDtypeStruct((m, d), F32),
        compiler_params=_cparams(("parallel",)),
        name="mixout",
    )(x, mix_a, mix_b, g, wo)


def _ple_kernel(x_ref, p_ref, g_ref, wpp_ref, wpg_ref, o_ref):
    x = x_ref[...]
    gate = jax.nn.sigmoid(_dot(_rms(x, g_ref[NG_PLE_PRE:NG_PLE_PRE + 1, :]).astype(BF16), wpg_ref[...]))
    proj = _dot(p_ref[...].astype(BF16), wpp_ref[...])
    o_ref[...] = x + _rms(proj * gate, g_ref[NG_PLE_POST:NG_PLE_POST + 1, :])


def _ple(x, p, g, wpp, wpg):
    m, d = x.shape
    tm = _pick_tile(m, 512)
    return pl.pallas_call(
        _ple_kernel,
        grid=(m // tm,),
        in_specs=[
            pl.BlockSpec((tm, d), lambda i: (i, 0)),
            pl.BlockSpec((tm, p.shape[1]), lambda i: (i, 0)),
            pl.BlockSpec(g.shape, lambda i: (0, 0)),
            pl.BlockSpec(wpp.shape, lambda i: (0, 0)),
            pl.BlockSpec(wpg.shape, lambda i: (0, 0)),
        ],
        out_specs=pl.BlockSpec((tm, d), lambda i: (i, 0)),
        out_shape=jax.ShapeDtypeStruct((m, d), F32),
        compiler_params=_cparams(("parallel",)),
        name="ple",
    )(x, p, g, wpp, wpg)


def _pair_masks(q):
    lane = _iota(q.shape, 1)
    return (jnp.where(lane < HEAD_DIM, q, 0.0).astype(BF16), jnp.where(lane >= HEAD_DIM, q, 0.0).astype(BF16))


def _sb_prompt_kernel(q_ref, k_ref, vt_ref, o_ref, acc_scr, c_scr, *, tq):
    i = pl.program_id(2)
    qm = _pair_masks(q_ref[...] * HEAD_DIM ** -0.5)
    row = _iota((tq, tq), 0)
    col = _iota((tq, tq), 1)
    later = jnp.where(col > row, 1.0, 0.0).astype(BF16)
    acc_scr[...] = jnp.zeros_like(acc_scr)
    c_scr[...] = jnp.zeros_like(c_scr)

    def blocks(kbs, diag):
        tiles = [(n, h) for n in range(len(kbs)) for h in range(2)]
        starts = [pl.multiple_of(kb * tq, tq) for kb in kbs]
        kblk = [k_ref[pl.ds(ks, tq), :].astype(BF16) for ks in starts]
        z = {(n, h): _dot_nt(kblk[n], qm[h]) for n, h in tiles}
        beta, neg_l1m = {}, {}
        for tl in tiles:
            e = jnp.exp(-jnp.abs(z[tl]))
            t1 = 1.0 + e
            r = 1.0 / t1
            nl = jnp.maximum(z[tl], 0.0) + jnp.log(t1)
            beta[tl] = jnp.where(z[tl] >= 0.0, r, e * r)
            neg_l1m[tl] = jnp.where(row < col, nl, 0.0) if diag else nl
        aft = {tl: _dot(later, neg_l1m[tl].astype(BF16)) for tl in tiles}
        w = {}
        for h in range(2):
            c = c_scr[h]
            for n in range(len(kbs)):
                tl = (n, h)
                wt = beta[tl] * jnp.exp(-(aft[tl] + c))
                w[tl] = (jnp.where(row < col, wt, 0.0) if diag else wt).astype(BF16)
                c = c + aft[tl][0:1, :] + neg_l1m[tl][0:1, :]
            c_scr[h] = c
        for h in range(2):
            upd = acc_scr[h]
            for n in range(len(kbs)):
                vth = vt_ref[h * HEAD_DIM:(h + 1) * HEAD_DIM, pl.ds(starts[n], tq)].astype(BF16)
                upd = upd + _dot(vth, w[(n, h)])
            acc_scr[h] = upd

    blocks([i], True)

    @pl.when(i % 2 == 1)
    def _():
        blocks([i - 1], False)

    def body(j, carry):
        kb = i - 1 - (i % 2) - 2 * j
        blocks([kb, kb - 1], False)
        return carry

    lax.fori_loop(0, i // 2, body, 0)
    o_ref[0:HEAD_DIM, :] = acc_scr[0]
    o_ref[HEAD_DIM:, :] = acc_scr[1]


def _sb_prompt(q, kv, vt, nb, t):
    tq = _pick_tile(t, 256)
    nq = t // tq
    npair = q.shape[1] // LANES
    return pl.pallas_call(
        functools.partial(_sb_prompt_kernel, tq=tq),
        grid=(nb, npair, nq),
        in_specs=[
            pl.BlockSpec((tq, LANES), lambda b, p, i: (b * nq + i, p)),
            pl.BlockSpec((t, LANES), lambda b, p, i: (b, p)),
            pl.BlockSpec((LANES, t), lambda b, p, i: (p, b)),
        ],
        out_specs=pl.BlockSpec((LANES, tq), lambda b, p, i: (p, b * nq + i)),
        out_shape=jax.ShapeDtypeStruct(vt.shape, F32),
        scratch_shapes=[pltpu.VMEM((2, HEAD_DIM, tq), F32), pltpu.VMEM((2, 1, tq), F32)],
        compiler_params=_cparams(("parallel", "parallel", "arbitrary")),
        name="sb_prompt",
    )(q, kv, vt)


def _cumsum_kernel(x_ref, o_ref, carry_scr, *, tc):
    @pl.when(pl.program_id(1) == 0)
    def _():
        carry_scr[...] = jnp.zeros_like(carry_scr)

    row = _iota((tc, tc), 0)
    col = _iota((tc, tc), 1)
    lower = jnp.where(row >= col, 1.0, 0.0).astype(BF16)
    hi, mid, lo = _split3(x_ref[...])
    cs = _dot(lower, hi) + _dot(lower, mid) + _dot(lower, lo) + carry_scr[...]
    carry_scr[...] = cs[tc - 1:tc, :]
    c_hi = cs.astype(BF16).astype(F32)
    c_mid = (cs - c_hi).astype(BF16).astype(F32)
    c_lo = cs - c_hi - c_mid
    lane = _iota(cs.shape, 1)
    o_ref[...] = jnp.where(lane < H_FOX, c_hi, jnp.where(lane < 2 * H_FOX, c_mid,
                                                          jnp.where(lane < 3 * H_FOX, c_lo, 0.0)))


def _cumsum_rows(x, nb, t):
    tc = _pick_tile(t, 512)
    nt = t // tc
    return pl.pallas_call(
        functools.partial(_cumsum_kernel, tc=tc),
        grid=(nb, nt),
        in_specs=[pl.BlockSpec((tc, LANES), lambda b, j: (b * nt + j, 0))],
        out_specs=pl.BlockSpec((tc, LANES), lambda b, j: (b * nt + j, 0)),
        out_shape=jax.ShapeDtypeStruct(x.shape, F32),
        scratch_shapes=[pltpu.VMEM((1, LANES), F32)],
        compiler_params=_cparams(("parallel", "arbitrary")),
        name="cumsum_rows",
    )(x)


def _softmax_tiles(logits, values, slots, acc_scr, m_scr, l_scr):
    staged = []
    for s, h in zip(logits, slots):
        m_old = m_scr[h]
        m_new = jnp.maximum(m_old, jnp.max(s, axis=0, keepdims=True))
        alpha = jnp.exp(m_old - m_new)
        pr = jnp.exp(s - m_new)
        l_scr[h] = alpha * l_scr[h] + jnp.sum(pr, axis=0, keepdims=True)
        m_scr[h] = m_new
        staged.append((alpha, pr.astype(BF16)))
    for (alpha, pr), vth, h in zip(staged, values, slots):
        acc_scr[h] = alpha * acc_scr[h] + _dot(vth, pr)


def _fox_prompt_kernel(q_ref, k_ref, vt_ref, cs_ref, o_ref, acc_scr, m_scr, l_scr, *, tq):
    p = pl.program_id(1)
    i = pl.program_id(2)
    qs = q_ref[...] * HEAD_DIM ** -0.5
    lane = _iota((tq, LANES), 1)
    qaug = []
    for h in range(2):
        head = 2 * p + h
        pick = jnp.where((lane == head) | (lane == head + H_FOX) | (lane == head + 2 * H_FOX), -1.0, 0.0)
        own = (lane < HEAD_DIM) if h == 0 else (lane >= HEAD_DIM)
        qaug.append(jnp.concatenate([jnp.where(own, qs, 0.0), pick], axis=1).astype(BF16))
    row = _iota((tq, tq), 0)
    col = _iota((tq, tq), 1)
    acc_scr[...] = jnp.zeros_like(acc_scr)
    m_scr[...] = jnp.full_like(m_scr, NEG_BIG)
    l_scr[...] = jnp.zeros_like(l_scr)

    def block(ks, nk, diag):
        kaug = jnp.concatenate([k_ref[pl.ds(ks, nk), :], cs_ref[pl.ds(ks, nk), :]], axis=1).astype(BF16)
        logits = [_dot_nt(kaug, qaug[h]) for h in range(2)]
        if diag:
            logits = [jnp.where(row <= col, s, NEG_BIG) for s in logits]
        values = [vt_ref[h * HEAD_DIM:(h + 1) * HEAD_DIM, pl.ds(ks, nk)].astype(BF16) for h in range(2)]
        _softmax_tiles(logits, values, (0, 1), acc_scr, m_scr, l_scr)

    block(pl.multiple_of(i * tq, tq), tq, True)

    @pl.when(i % 2 == 1)
    def _():
        block(pl.multiple_of((i - 1) * tq, tq), tq, False)

    def body(j, carry):
        block(pl.multiple_of(j * 2 * tq, 2 * tq), 2 * tq, False)
        return carry

    lax.fori_loop(0, i // 2, body, 0)
    o_ref[0:HEAD_DIM, :] = acc_scr[0] / l_scr[0]
    o_ref[HEAD_DIM:, :] = acc_scr[1] / l_scr[1]


def _fox_prompt(q, kv, vt, cum_split, nb, t):
    tq = _pick_tile(t, 256)
    nq = t // tq
    npair = q.shape[1] // LANES
    return pl.pallas_call(
        functools.partial(_fox_prompt_kernel, tq=tq),
        grid=(nb, npair, nq),
        in_specs=[
            pl.BlockSpec((tq, LANES), lambda b, p, i: (b * nq + i, p)),
            pl.BlockSpec((t, LANES), lambda b, p, i: (b, p)),
            pl.BlockSpec((LANES, t), lambda b, p, i: (p, b)),
            pl.BlockSpec((t, LANES), lambda b, p, i: (b, 0)),
        ],
        out_specs=pl.BlockSpec((LANES, tq), lambda b, p, i: (p, b * nq + i)),
        out_shape=jax.ShapeDtypeStruct(vt.shape, F32),
        scratch_shapes=[pltpu.VMEM((2, HEAD_DIM, tq), F32), pltpu.VMEM((2, 1, tq), F32),
                        pltpu.VMEM((2, 1, tq), F32)],
        compiler_params=_cparams(("parallel", "parallel", "arbitrary")),
        name="fox_prompt",
    )(q, kv, vt, cum_split)


def _gla_kernel(qkv_ref, la_ref, r_ref, s0_ref, gn_ref, o_ref, sfin_ref, st_scr, o_scr, e_scr, *, chunk, tt):
    j = pl.program_id(1)
    nk = H_GLA * DK_GLA
    nv = H_GLA * DV_GLA
    c = chunk

    @pl.when(j == 0)
    def _():
        st_scr[...] = s0_ref[0]

    expand = jnp.where(_iota((nk, nv), 0) // DK_GLA == _iota((nk, nv), 1) // DV_GLA, 1.0, 0.0).astype(BF16)
    diag_t = jnp.where(_iota((nv, nk), 0) // DV_GLA == _iota((nv, nk), 1) // DK_GLA, 1.0, 0.0)
    rowc = _iota((c, nk), 0)

    def chunk_body(ci, carry):
        r0 = pl.multiple_of(ci * c, c)
        q = qkv_ref[0, pl.ds(r0, c), 0:nk] * DK_GLA ** -0.5
        k = qkv_ref[0, pl.ds(r0, c), nk:2 * nk]
        v = qkv_ref[0, pl.ds(r0, c), 2 * nk:2 * nk + nv]
        g = la_ref[0, pl.ds(r0, c), :]
        b = jnp.zeros((c, nk), F32)
        for s in range(c):
            b = b + jnp.where(rowc >= s, g[s:s + 1, :], 0.0)
        for s in range(c):
            decay = jnp.where(rowc >= s, jnp.exp(b - b[s:s + 1, :]), 0.0)
            e_scr[s * c:(s + 1) * c, :] = decay * q * k[s:s + 1, :]
        hi, lo = _split2(e_scr[...])
        att = _dot(hi, expand) + _dot(lo, expand)
        st = st_scr[...]
        o = _dot_nt((q * jnp.exp(b)).astype(BF16), st.astype(BF16))
        for s in range(c):
            o = o + att[s * c:(s + 1) * c, :] * v[s:s + 1, :]
        o_scr[pl.ds(r0, c), :] = o
        b_last = b[c - 1:c, :]
        kd = k * jnp.exp(b_last - b)
        st_scr[...] = st * jnp.exp(b_last) + diag_t * _dot_tn(v.astype(BF16), kd.astype(BF16))
        return carry

    lax.fori_loop(0, tt // c, chunk_body, 0)

    o = o_scr[...]
    r = r_ref[0]
    gn = gn_ref[...]
    for h in range(H_GLA):
        sl = slice(h * DV_GLA, (h + 1) * DV_GLA)
        oh = o[:, sl]
        rh = r[:, sl]
        y = oh * lax.rsqrt(jnp.mean(oh * oh, axis=-1, keepdims=True) + EPS) * gn
        o_ref[0, :, sl] = y * (rh * jax.nn.sigmoid(rh))

    @pl.when(j == pl.num_programs(1) - 1)
    def _():
        sfin_ref[0] = st_scr[...]


def _gla(qkv, la, r, s0_t, gnorm):
    nb, t, _ = qkv.shape
    chunk = min(GLA_CHUNK, t)
    tt = _pick_tile(t, 512)
    nk, nv = H_GLA * DK_GLA, H_GLA * DV_GLA
    return pl.pallas_call(
        functools.partial(_gla_kernel, chunk=chunk, tt=tt),
        grid=(nb, t // tt),
        in_specs=[
            pl.BlockSpec((1, tt, qkv.shape[2]), lambda b, j: (b, j, 0)),
            pl.BlockSpec((1, tt, nk), lambda b, j: (b, j, 0)),
            pl.BlockSpec((1, tt, nv), lambda b, j: (b, j, 0)),
            pl.BlockSpec((1, nv, nk), lambda b, j: (b, 0, 0)),
            pl.BlockSpec(gnorm.shape, lambda b, j: (0, 0)),
        ],
        out_specs=[
            pl.BlockSpec((1, tt, nv), lambda b, j: (b, j, 0)),
            pl.BlockSpec((1, nv, nk), lambda b, j: (b, 0, 0)),
        ],
        out_shape=[jax.ShapeDtypeStruct((nb, t, nv), F32), jax.ShapeDtypeStruct(s0_t.shape, F32)],
        scratch_shapes=[pltpu.VMEM((nv, nk), F32), pltpu.VMEM((tt, nv), F32), pltpu.VMEM((chunk * chunk, nk), F32)],
        compiler_params=_cparams(("parallel", "arbitrary")),
        name="gla",
    )(qkv, la, r, s0_t, gnorm)


def _t5_bucket(dist):
    max_exact = N_BUCKETS // 2
    d = jnp.maximum(dist, 0)
    large = max_exact + (jnp.log(jnp.maximum(d, 1).astype(F32) / max_exact)
                         / math.log(MAX_DISTANCE / max_exact) * (N_BUCKETS - max_exact)).astype(I32)
    return jnp.where(d < max_exact, d, jnp.minimum(large, N_BUCKETS - 1))


def _bias_tables_kernel(rb_ref, bp_ref, bs_ref, *, tq, past):
    bucket_p = _t5_bucket(_iota((2 * tq, tq), 1) - _iota((2 * tq, tq), 0) + tq)
    bucket_s = _t5_bucket(past + _iota(bs_ref.shape, 0) // H_DSA - _iota(bs_ref.shape, 1))
    head_s = _iota(bs_ref.shape, 0) % H_DSA
    acc_s = jnp.zeros(bs_ref.shape, F32)
    for h in range(H_DSA):
        acc_p = jnp.zeros((2 * tq, tq), F32)
        for bkt in range(N_BUCKETS):
            val = rb_ref[bkt, h]
            acc_p = jnp.where(bucket_p == bkt, val, acc_p)
            acc_s = jnp.where((bucket_s == bkt) & (head_s == h), val, acc_s)
        bp_ref[h] = acc_p
    bs_ref[...] = acc_s


def _bias_tables(rel_bias, tq, past, n_new):
    cols_s = past + LANES
    return pl.pallas_call(
        functools.partial(_bias_tables_kernel, tq=tq, past=past),
        in_specs=[pl.BlockSpec(memory_space=pltpu.SMEM)],
        out_specs=[pl.BlockSpec(memory_space=pltpu.VMEM), pl.BlockSpec(memory_space=pltpu.VMEM)],
        out_shape=[jax.ShapeDtypeStruct((H_DSA, 2 * tq, tq), F32),
                   jax.ShapeDtypeStruct((n_new * H_DSA, cols_s), F32)],
        compiler_params=pltpu.CompilerParams(vmem_limit_bytes=VMEM_LIMIT_BYTES),
        name="bias_tables",
    )(rel_bias)


def _sort_key(score):
    bits = lax.bitcast_convert_type(score + 0.0, I32)
    return bits ^ ((bits >> 31) & 0x7FFFFFFF)


def _select_threshold(key_scr, n_chunks, chunk, topk, n_queries, key_axis):
    if key_axis == 1:
        chunk_shape, vec_shape = (n_queries, chunk), (n_queries, 1)

        def chunk_slice(cb):
            return (slice(None), pl.ds(pl.multiple_of(cb * chunk, chunk), chunk))

        def count(pred):
            def body(cb, acc):
                idx = cb * chunk + _iota(chunk_shape, 1)
                return acc + jnp.where(pred(key_scr[chunk_slice(cb)], idx), 1.0, 0.0)

            acc = lax.fori_loop(0, n_chunks, body, jnp.zeros(chunk_shape, F32))
            return jnp.sum(acc, axis=1, keepdims=True)
    else:
        chunk_shape, vec_shape = (chunk, n_queries), (1, n_queries)

        def chunk_slice(cb):
            return (pl.ds(pl.multiple_of(cb * chunk, chunk), chunk), slice(None))

        def count(pred):
            def body(cb, acc):
                idx = cb * chunk + _iota(chunk_shape, 0)
                hit = jnp.where(pred(key_scr[chunk_slice(cb)], idx), 1.0, 0.0)
                return acc + jnp.sum(hit.reshape(chunk // 8, 8, n_queries), axis=0)

            acc = lax.fori_loop(0, n_chunks, body, jnp.zeros((8, n_queries), F32))
            return jnp.sum(acc, axis=0, keepdims=True)

    kf = float(topk)
    thr0 = jnp.where(count(lambda kk, idx: kk >= 0) >= kf, 0, INT_MIN).astype(I32)

    def bit_body(t, thr):
        cand = thr + (jnp.int32(1) << (30 - t))
        return jnp.where(count(lambda kk, idx: kk >= cand) >= kf, cand, thr)

    thr = lax.fori_loop(0, 31, bit_body, thr0)
    n_gt = count(lambda kk, idx: kk > thr)
    n_eq = count(lambda kk, idx: kk == thr)
    need = kf - n_gt
    trim = (n_eq > need) & (thr > INT_MIN)

    @pl.when(jnp.max(jnp.where(trim, 1.0, 0.0)) > 0.0)
    def _():
        n_bits = max(1, int(math.ceil(math.log2(key_scr.shape[key_axis] + 1))))

        def idx_body(t, cut):
            cand = cut + (jnp.int32(1) << (n_bits - 1 - t))
            return jnp.where(count(lambda kk, idx: (kk == thr) & (idx < cand)) < need, cand, cut)

        cut = lax.fori_loop(0, n_bits, idx_body, jnp.zeros(vec_shape, I32))

        def demote(cb, carry):
            kk = key_scr[chunk_slice(cb)]
            idx = cb * chunk + _iota(chunk_shape, key_axis)
            key_scr[chunk_slice(cb)] = jnp.where(trim & (kk == thr) & (idx > cut), kk - 1, kk)
            return carry

        lax.fori_loop(0, n_chunks, demote, 0)

    return jnp.maximum(thr, INT_MIN + 1)


def _dsa_prompt_kernel(rb_ref, q_ref, iq_ref, iwt_ref, k_ref, vt_ref, ik_ref, bias_ref, o_ref,
                       key_scr, acc_scr, m_scr, l_scr, *, tq, topk):
    i = pl.program_id(1)
    npair = H_DSA // 2
    row = _iota((tq, tq), 0)
    col = _iota((tq, tq), 1)

    iq = iq_ref[...]
    iqm = []
    for p in range(H_IDX // 2):
        iqm.extend(_pair_masks(iq[:, p * LANES:(p + 1) * LANES]))
    wq = [iwt_ref[MISC_IW + h:MISC_IW + h + 1, :] * H_IDX ** -0.5 for h in range(H_IDX)]

    def score_block(kb, diag):
        ks = pl.multiple_of(kb * tq, tq)
        ik2 = ik_ref[pl.ds(ks, tq), :]
        score = jnp.zeros((tq, tq), F32)
        for h in range(H_IDX):
            score = score + jnp.maximum(_dot_nt(ik2, iqm[h]), 0.0) * wq[h]
        key = _sort_key(score)
        if diag:
            key = jnp.where(row <= col, key, INT_MIN)
        key_scr[pl.ds(ks, tq), :] = key

    score_block(i, True)

    def score_body(kb, carry):
        score_block(kb, False)
        return carry

    lax.fori_loop(0, i, score_body, 0)

    thr = _select_threshold(key_scr, i + 1, tq, topk, tq, key_axis=0)

    q = q_ref[...] * HEAD_DIM ** -0.5
    qm = []
    for p in range(npair):
        qm.extend(_pair_masks(q[:, p * LANES:(p + 1) * LANES]))
    acc_scr[...] = jnp.zeros_like(acc_scr)
    m_scr[...] = jnp.full_like(m_scr, NEG_BIG)
    l_scr[...] = jnp.zeros_like(l_scr)

    def attend_block(kb, near):
        ks = pl.multiple_of(kb * tq, tq)
        sel = key_scr[pl.ds(ks, tq), :] >= thr
        logits, values = [], []
        for p in range(npair):
            kblk = k_ref[pl.ds(ks, tq), p * LANES:(p + 1) * LANES]
            for hh in range(2):
                h = 2 * p + hh
                if near == 0:
                    bias = rb_ref[N_BUCKETS - 1, h]
                else:
                    bias = bias_ref[h, (near - 1) * tq:near * tq, :]
                logits.append(jnp.where(sel, _dot_nt(kblk, qm[h]) + bias, NEG_BIG))
                values.append(vt_ref[h * HEAD_DIM:(h + 1) * HEAD_DIM, pl.ds(ks, tq)])
        _softmax_tiles(logits, values, tuple(range(H_DSA)), acc_scr, m_scr, l_scr)

    attend_block(i, 2)

    @pl.when(i >= 1)
    def _():
        attend_block(i - 1, 1)

    def attend_body(kb, carry):
        attend_block(kb, 0)
        return carry

    lax.fori_loop(0, jnp.maximum(i - 1, 0), attend_body, 0)
    for h in range(H_DSA):
        o_ref[h * HEAD_DIM:(h + 1) * HEAD_DIM, :] = acc_scr[h] / l_scr[h]


def _dsa_prompt(rel_bias, q, iq, iw_t, k_bf16, vt_bf16, ik2_bf16, bias_p, nb, t, tq):
    nq = t // tq
    topk = min(DSA_TOPK, t // 4)
    single = pl.Buffered(1)
    return pl.pallas_call(
        functools.partial(_dsa_prompt_kernel, tq=tq, topk=topk),
        grid=(nb, nq),
        in_specs=[
            pl.BlockSpec(memory_space=pltpu.SMEM),
            pl.BlockSpec((tq, q.shape[1]), lambda b, i: (b * nq + i, 0)),
            pl.BlockSpec((tq, iq.shape[1]), lambda b, i: (b * nq + i, 0)),
            pl.BlockSpec((LANES, tq), lambda b, i: (0, b * nq + i)),
            pl.BlockSpec((t, k_bf16.shape[1]), lambda b, i: (b, 0), pipeline_mode=single),
            pl.BlockSpec((vt_bf16.shape[0], t), lambda b, i: (0, b), pipeline_mode=single),
            pl.BlockSpec((t, LANES), lambda b, i: (b, 0), pipeline_mode=single),
            pl.BlockSpec(bias_p.shape, lambda b, i: (0, 0, 0), pipeline_mode=single),
        ],
        out_specs=pl.BlockSpec((vt_bf16.shape[0], tq), lambda b, i: (0, b * nq + i)),
        out_shape=jax.ShapeDtypeStruct(vt_bf16.shape, F32),
        scratch_shapes=[pltpu.VMEM((t, tq), I32), pltpu.VMEM((H_DSA, HEAD_DIM, tq), F32),
                        pltpu.VMEM((H_DSA, 1, tq), F32), pltpu.VMEM((H_DSA, 1, tq), F32)],
        compiler_params=_cparams(("parallel", "arbitrary")),
        name="dsa_prompt",
    )(rel_bias, q, iq, iw_t, k_bf16, vt_bf16, ik2_bf16, bias_p)


def _rows_by_head(x, n_heads):
    t, w = x.shape
    head_mask = _iota((n_heads, w), 1) // HEAD_DIM == _iota((n_heads, w), 0)
    return jnp.concatenate([jnp.where(head_mask, x[i:i + 1, :], 0.0) for i in range(t)], axis=0)


def _collapse_heads(acc, n_new, n_heads):
    w = acc.shape[1]
    head_mask = _iota((n_heads, w), 1) // HEAD_DIM == _iota((n_heads, w), 0)
    rows = [jnp.sum(jnp.where(head_mask, acc[i * n_heads:(i + 1) * n_heads, :], 0.0), axis=0, keepdims=True)
            for i in range(n_new)]
    return jnp.concatenate(rows, axis=0)


def _sb_sample_kernel(pt_ref, q_ref, kvn_ref, *refs, n_pages, n_new):
    del pt_ref
    kt_pages = refs[:n_pages]
    vt_pages = refs[n_pages:2 * n_pages]
    o_ref = refs[2 * n_pages]
    hw = H_SB * HEAD_DIM
    qbd = _rows_by_head(q_ref[0] * HEAD_DIM ** -0.5, H_SB)
    qbd_b = qbd.astype(BF16)
    nr = n_new * H_SB
    trow = _iota((nr, 1), 0) // H_SB
    kn = kvn_ref[0, :, :hw]
    vn = kvn_ref[0, :, hw:]
    lg_new, z_new = [], []
    for j in range(n_new):
        z = jnp.sum(qbd * kn[j:j + 1, :], axis=1, keepdims=True)
        z_new.append(z)
        lg_new.append(jnp.where(j < trow, _log_sigmoid(-z), 0.0))
    acc = jnp.zeros((nr, hw), F32)
    carry = jnp.zeros((nr, 1), F32)
    for j in reversed(range(n_new)):
        w = jnp.where(j < trow, jnp.exp(z_new[j] + lg_new[j] + carry), 0.0)
        acc = acc + w * vn[j:j + 1, :]
        carry = carry + lg_new[j]
    upper = jnp.where(_iota((LANES, LANES), 0) > _iota((LANES, LANES), 1), 1.0, 0.0).astype(BF16)
    z = [_dot(qbd_b, kt_pages[pg][0, 0, 0].astype(BF16)) for pg in range(n_pages)]
    lg = [_log_sigmoid(-zp) for zp in z]
    aft = []
    for lgp in lg:
        hi, lo = _split2(lgp)
        aft.append(_dot(hi, upper) + _dot(lo, upper))
    for pg in reversed(range(n_pages)):
        w = jnp.exp(z[pg] + lg[pg] + aft[pg] + carry)
        acc = acc + _dot_nt(w.astype(BF16), vt_pages[pg][0, 0, 0].astype(BF16))
        carry = carry + aft[pg][:, 0:1] + lg[pg][:, 0:1]
    o_ref[0] = _collapse_heads(acc, n_new, H_SB)


def _page_specs(n_pages, layer, block, tail):
    def spec(j):
        return pl.BlockSpec(block, lambda b, pt: (layer, pt[b, j]) + tail)
    return [spec(j) for j in range(n_pages)]


def _kv_page_specs(n_pages, layer, cache_t):
    block = (1, 1, 1) + cache_t.shape[3:]
    return _page_specs(n_pages, layer, block, (0, 0, 0)) + _page_specs(n_pages, layer, block, (1, 0, 0))


def _sb_sample(page_table, q, kv_new, cache_t, layer):
    db, n_new, hw = q.shape
    n_pages = page_table.shape[1]
    grid_spec = pltpu.PrefetchScalarGridSpec(
        num_scalar_prefetch=1,
        grid=(db,),
        in_specs=[pl.BlockSpec((1, n_new, hw), lambda b, pt: (b, 0, 0)),
                  pl.BlockSpec((1, n_new, 2 * hw), lambda b, pt: (b, 0, 0))]
        + _kv_page_specs(n_pages, layer, cache_t),
        out_specs=pl.BlockSpec((1, n_new, hw), lambda b, pt: (b, 0, 0)),
    )
    return pl.pallas_call(
        functools.partial(_sb_sample_kernel, n_pages=n_pages, n_new=n_new),
        grid_spec=grid_spec,
        out_shape=jax.ShapeDtypeStruct(q.shape, F32),
        compiler_params=_cparams(("parallel",)),
        name="sb_sample",
    )(page_table, q, kv_new, *([cache_t] * (2 * n_pages)))


def _head_column(x, n_new, n_heads, pick):
    lane_is_head = _iota((n_heads, x.shape[1]), 1) == _iota((n_heads, x.shape[1]), 0)
    cols = [jnp.sum(jnp.where(lane_is_head, x[pick(i):pick(i) + 1, :], 0.0), axis=1, keepdims=True)
            for i in range(n_new)]
    return jnp.concatenate(cols, axis=0)


def _fox_sample_kernel(pt_ref, q_ref, kvn_ref, lfn_ref, *refs, n_pages, n_new):
    del pt_ref
    kt_pages = refs[:n_pages]
    vt_pages = refs[n_pages:2 * n_pages]
    lf_pages = refs[2 * n_pages:3 * n_pages]
    o_ref = refs[3 * n_pages]
    hw = H_FOX * HEAD_DIM
    qbd = _rows_by_head(q_ref[0] * HEAD_DIM ** -0.5, H_FOX)
    qbd_b = qbd.astype(BF16)
    nr = n_new * H_FOX
    trow = _iota((nr, 1), 0) // H_FOX
    kn = kvn_ref[0, :, :hw]
    vn = kvn_ref[0, :, hw:]
    lfn = lfn_ref[0]
    csum = [lfn[0:1, :]]
    for j in range(1, n_new):
        csum.append(csum[-1] + lfn[j:j + 1, :])
    csum = jnp.concatenate(csum, axis=0)
    c_query = _head_column(csum, n_new, H_FOX, lambda t: t)
    s_new = []
    for j in range(n_new):
        c_key = _head_column(csum, n_new, H_FOX, lambda t, j=j: j)
        z = jnp.sum(qbd * kn[j:j + 1, :], axis=1, keepdims=True)
        s_new.append(jnp.where(j <= trow, z + (c_query - c_key), NEG_BIG))
    upper = jnp.where(_iota((LANES, LANES), 0) > _iota((LANES, LANES), 1), 1.0, 0.0).astype(BF16)
    z = [_dot(qbd_b, kt_pages[pg][0, 0, 0].astype(BF16)) for pg in range(n_pages)]
    lf, aft = [], []
    for pg in range(n_pages):
        lfp = jnp.concatenate([lf_pages[pg][0, 0]] * n_new, axis=0)
        hi, mid, lo = _split3(lfp)
        lf.append(lfp)
        aft.append(_dot(hi, upper) + _dot(mid, upper) + _dot(lo, upper))
    carry = c_query
    s = [None] * n_pages
    for pg in reversed(range(n_pages)):
        s[pg] = z[pg] + aft[pg] + carry
        carry = carry + aft[pg][:, 0:1] + lf[pg][:, 0:1]
    m = s_new[0]
    for j in range(1, n_new):
        m = jnp.maximum(m, s_new[j])
    for pg in range(n_pages):
        m = jnp.maximum(m, jnp.max(s[pg], axis=1, keepdims=True))
    l = jnp.zeros((nr, 1), F32)
    acc = jnp.zeros((nr, hw), F32)
    for j in range(n_new):
        pr = jnp.exp(s_new[j] - m)
        l = l + pr
        acc = acc + pr * vn[j:j + 1, :]
    for pg in range(n_pages):
        pr = jnp.exp(s[pg] - m)
        l = l + jnp.sum(pr, axis=1, keepdims=True)
        acc = acc + _dot_nt(pr.astype(BF16), vt_pages[pg][0, 0, 0].astype(BF16))
    o_ref[0] = _collapse_heads(acc / l, n_new, H_FOX)


def _fox_sample(page_table, q, kv_new, lf_new, cache_t, cache_lf_t, layer):
    db, n_new, hw = q.shape
    n_pages = page_table.shape[1]
    page = cache_lf_t.shape[3]
    grid_spec = pltpu.PrefetchScalarGridSpec(
        num_scalar_prefetch=1,
        grid=(db,),
        in_specs=[pl.BlockSpec((1, n_new, hw), lambda b, pt: (b, 0, 0)),
                  pl.BlockSpec((1, n_new, 2 * hw), lambda b, pt: (b, 0, 0)),
                  pl.BlockSpec((1, n_new, LANES), lambda b, pt: (b, 0, 0))]
        + _kv_page_specs(n_pages, layer, cache_t)
        + _page_specs(n_pages, layer, (1, 1, H_FOX, page), (0, 0)),
        out_specs=pl.BlockSpec((1, n_new, hw), lambda b, pt: (b, 0, 0)),
    )
    return pl.pallas_call(
        functools.partial(_fox_sample_kernel, n_pages=n_pages, n_new=n_new),
        grid_spec=grid_spec,
        out_shape=jax.ShapeDtypeStruct(q.shape, F32),
        compiler_params=_cparams(("parallel",)),
        name="fox_sample",
    )(page_table, q, kv_new, lf_new, *([cache_t] * (2 * n_pages)), *([cache_lf_t] * n_pages))


def _dsa_sample_kernel(pt_ref, q_ref, kvn_ref, iq_ref, iw_ref, ikn_ref, bias_ref, *refs, n_pages, n_new, topk):
    del pt_ref
    kt_pages = refs[:n_pages]
    vt_pages = refs[n_pages:2 * n_pages]
    ik_pages = refs[2 * n_pages:3 * n_pages]
    o_ref = refs[3 * n_pages]
    key_scr = refs[3 * n_pages + 1]
    hw = H_DSA * HEAD_DIM
    nr = n_new * H_DSA
    page = ik_pages[0].shape[3]
    past = n_pages * page
    trow = _iota((nr, LANES), 0) // H_DSA
    lane = _iota((nr, LANES), 1)

    iq = iq_ref[0].astype(BF16)
    iw = iw_ref[0] * H_IDX ** -0.5

    def head_sum(x):
        cols = x.shape[1]
        g = jnp.sum(x.reshape(n_new, H_IDX, cols), axis=1, keepdims=True)
        return jnp.broadcast_to(g, (n_new, H_IDX, cols)).reshape(nr, cols)

    for pg in range(n_pages):
        ikp = ik_pages[pg][0, 0].astype(BF16)
        sc = head_sum(jnp.maximum(_dot(iq, ikp), 0.0) * iw)
        key_scr[:, pg * page:(pg + 1) * page] = _sort_key(sc)
    ikn = ikn_ref[0].astype(BF16)
    sc = head_sum(jnp.maximum(_dot(iq, ikn), 0.0) * iw)
    key_scr[:, past:past + LANES] = jnp.where(lane <= trow, _sort_key(sc), INT_MIN)

    thr = _select_threshold(key_scr, 1, past + LANES, topk, nr, key_axis=1)

    qbd_b = _rows_by_head(q_ref[0] * HEAD_DIM ** -0.5, H_DSA).astype(BF16)
    def keys_t(pg):
        return (kt_pages[pg][0, 0, 0] if pg < n_pages else kvn_ref[0, 0]).astype(BF16)

    def values_t(pg):
        return (vt_pages[pg][0, 0, 0] if pg < n_pages else kvn_ref[0, 1]).astype(BF16)

    s = []
    for pg in range(n_pages + 1):
        sel = key_scr[:, pg * page:(pg + 1) * page] >= thr
        s.append(jnp.where(sel, _dot(qbd_b, keys_t(pg)) + bias_ref[:, pg * page:(pg + 1) * page], NEG_BIG))
    m = jnp.max(s[0], axis=1, keepdims=True)
    for pg in range(1, n_pages + 1):
        m = jnp.maximum(m, jnp.max(s[pg], axis=1, keepdims=True))
    l = jnp.zeros((nr, 1), F32)
    acc = jnp.zeros((nr, hw), F32)
    for pg in range(n_pages + 1):
        pr = jnp.exp(s[pg] - m)
        l = l + jnp.sum(pr, axis=1, keepdims=True)
        acc = acc + _dot_nt(pr.astype(BF16), values_t(pg))
    o_ref[0] = _collapse_heads(acc / l, n_new, H_DSA)


def _dsa_sample(page_table, q, kv_new_t, iq_rows, iw_rows, ik_new_t, bias_s, cache_t, cache_ik_t, layer):
    db, n_new, hw = q.shape
    n_pages = page_table.shape[1]
    page = cache_ik_t.shape[3]
    nr = n_new * H_DSA
    topk = min(DSA_TOPK, (n_pages * page + n_new) // 4)
    grid_spec = pltpu.PrefetchScalarGridSpec(
        num_scalar_prefetch=1,
        grid=(db,),
        in_specs=[pl.BlockSpec((1, n_new, hw), lambda b, pt: (b, 0, 0)),
                  pl.BlockSpec((1, 2, hw, page), lambda b, pt: (b, 0, 0, 0)),
                  pl.BlockSpec((1, nr, D_IDX), lambda b, pt: (b, 0, 0)),
                  pl.BlockSpec((1, nr, 1), lambda b, pt: (b, 0, 0)),
                  pl.BlockSpec((1, D_IDX, page), lambda b, pt: (b, 0, 0)),
                  pl.BlockSpec(bias_s.shape, lambda b, pt: (0, 0))]
        + _kv_page_specs(n_pages, layer, cache_t)
        + _page_specs(n_pages, layer, (1, 1, D_IDX, page), (0, 0)),
        out_specs=pl.BlockSpec((1, n_new, hw), lambda b, pt: (b, 0, 0)),
        scratch_shapes=[pltpu.VMEM((nr, n_pages * page + LANES), I32)],
    )
    return pl.pallas_call(
        functools.partial(_dsa_sample_kernel, n_pages=n_pages, n_new=n_new, topk=topk),
        grid_spec=grid_spec,
        out_shape=jax.ShapeDtypeStruct(q.shape, F32),
        compiler_params=_cparams(("parallel",)),
        name="dsa_sample",
    )(page_table, q, kv_new_t, iq_rows, iw_rows, ik_new_t, bias_s, *([cache_t] * (2 * n_pages)),
      *([cache_ik_t] * n_pages))


def _state_to_block_diag(s):
    nb = s.shape[0]
    eye = jnp.eye(H_GLA, dtype=s.dtype)
    return jnp.einsum("bhkv,hg->bhvgk", s, eye).reshape(nb, H_GLA * DV_GLA, H_GLA * DK_GLA)


def _block_diag_to_state(bd):
    nb = bd.shape[0]
    bd = bd.reshape(nb, H_GLA, DV_GLA, H_GLA, DK_GLA)
    return jnp.stack([jnp.swapaxes(bd[:, h, :, h, :], 1, 2) for h in range(H_GLA)], axis=1)


def _pad_cols(w, width):
    return jnp.pad(w, ((0, 0), (0, width - w.shape[1])))


def _cd_weight_layout(w):
    gq = H_GLA * DK_GLA
    gv = H_GLA * DV_GLA
    dh = H_DSA * HEAD_DIM
    offs = np.cumsum([0, gq, gq, gv, GLA_RANK, gv, dh, dh, dh, H_IDX * D_IDX, D_IDX, H_IDX])
    seg = [w[:, offs[i]:offs[i + 1]] for i in range(11)]
    q_g, k_g, v_g, g_lr, r_g, q_d, k_d, v_d, iq, ik, iw = seg
    misc = _pad_cols(jnp.concatenate([iw, g_lr], axis=1), LANES)
    return jnp.concatenate([q_g, k_g, v_g, r_g, q_d, k_d, v_d, iq, ik, ik, misc], axis=1)


def kernel(x_prompt, x_sample, p_prompt, p_sample, cache_sb_kv, cache_fox_kv, cache_fox_logf, state_gla,
           cache_dsa_kv, cache_dsa_idxk, page_table, norm_gains, ffn_w_gate, ffn_w_up, ffn_w_down, w_out,
           ple_w_proj, ple_w_gate, ab_w_in, ab_b_forget, cd_w_in, gla_w_gate_up, gla_b_gate, gla_norm, rel_bias):
    nb, t, d = x_prompt.shape
    db, n_new, _ = x_sample.shape
    depth = norm_gains.shape[0]
    bt = nb * t
    n_pages = page_table.shape[1]
    page = cache_sb_kv.shape[2]
    past = n_pages * page
    tq = _pick_tile(t, 256)

    x = jnp.concatenate([x_prompt.reshape(bt, d), x_sample.reshape(db * n_new, d)], axis=0)
    p_all = jnp.concatenate([p_prompt.reshape(depth, bt, -1), p_sample.reshape(depth, db * n_new, -1)], axis=1)

    wg = ffn_w_gate.astype(BF16)
    wu = ffn_w_up.astype(BF16)
    wd = ffn_w_down.astype(BF16)
    wo = w_out.astype(BF16)
    wpp = ple_w_proj.astype(BF16)
    wpg = ple_w_gate.astype(BF16)

    def pages_last(c):
        ct = jnp.transpose(c, (0, 1, 3, 4, 5, 2))
        return ct.reshape(ct.shape[:3] + (-1, ct.shape[5]))

    sb_cache = pages_last(cache_sb_kv)
    fox_cache = pages_last(cache_fox_kv)
    dsa_cache = pages_last(cache_dsa_kv)
    fox_lf_cache_t = jnp.swapaxes(cache_fox_logf, 2, 3)
    dsa_ik_cache_t = jnp.swapaxes(cache_dsa_idxk, 2, 3)

    bias_p, bias_s = _bias_tables(rel_bias, tq, past, n_new)

    def split(a):
        return a[:bt], a[bt:]

    outs = {k: [] for k in ("sb_p", "sb_s", "fkv_p", "fkv_s", "flf_p", "flf_s", "gs_p", "gs_s", "dkv_p", "dkv_s",
                            "dik_p", "dik_s")}
    for i in range(depth):
        j = i // 2
        g = norm_gains[i]
        x = _ffn(x, g, wg[i, 0], wu[i, 0], wd[i, 0], NG_FFN1_PRE, NG_FFN1_POST)
        if i % 2 == 0:
            w_f = ab_w_in[j][:, AB_F:]
            w_pad = _pad_cols(jnp.concatenate([ab_w_in[j], w_f, w_f], axis=1), AB_WIDTH_PAD).astype(BF16)
            b_pad = _pad_cols(jnp.tile(ab_b_forget[j], 3)[None, :], LANES)
            q_sb, kv_sb, q_fx, kv_fx, lf = _proj_ab(x, g, w_pad, b_pad)
            q_sb_p, q_sb_s = split(q_sb)
            kv_sb_p, kv_sb_s = split(kv_sb)
            q_fx_p, q_fx_s = split(q_fx)
            kv_fx_p, kv_fx_s = split(kv_fx)
            lf_p, lf_s = split(lf)
            cum_split = _cumsum_rows(lf_p, nb, t)
            half = kv_sb_p.shape[1] // 2
            o_sb_p = _sb_prompt(q_sb_p, kv_sb_p, kv_sb_p[:, half:].T, nb, t).T
            o_fx_p = _fox_prompt(q_fx_p, kv_fx_p, kv_fx_p[:, half:].T, cum_split, nb, t).T
            o_sb_s = _sb_sample(page_table, q_sb_s.reshape(db, n_new, -1), kv_sb_s.reshape(db, n_new, -1),
                                sb_cache, j)
            o_fx_s = _fox_sample(page_table, q_fx_s.reshape(db, n_new, -1), kv_fx_s.reshape(db, n_new, -1),
                                 lf_s.reshape(db, n_new, LANES), fox_cache, fox_lf_cache_t, j)
            mix_a = jnp.concatenate([o_sb_p, o_sb_s.reshape(db * n_new, -1)], axis=0)
            mix_b = jnp.concatenate([o_fx_p, o_fx_s.reshape(db * n_new, -1)], axis=0)
            outs["sb_p"].append(kv_sb_p.reshape(nb, t, 2, H_SB, HEAD_DIM))
            outs["sb_s"].append(kv_sb_s.reshape(db, n_new, 2, H_SB, HEAD_DIM))
            outs["fkv_p"].append(kv_fx_p.reshape(nb, t, 2, H_FOX, HEAD_DIM))
            outs["fkv_s"].append(kv_fx_s.reshape(db, n_new, 2, H_FOX, HEAD_DIM))
            outs["flf_p"].append(lf_p[:, :H_FOX].reshape(nb, t, H_FOX))
            outs["flf_s"].append(lf_s[:, :H_FOX].reshape(db, n_new, H_FOX))
        else:
            w_pad = _pad_cols(_cd_weight_layout(cd_w_in[j]), CD_WIDTH_PAD).astype(BF16)
            wgu_pad = jnp.zeros((LANES, H_GLA * DK_GLA), F32).at[MISC_GLR:MISC_GLR + GLA_RANK].set(
                gla_w_gate_up[j]).astype(BF16)
            qkv_g, r_g, q_d, kv_d, iq, ik2, misc, la = _proj_cd(x, g, w_pad, wgu_pad, gla_b_gate[j][None, :])
            gnorm = gla_norm[j][None, :]
            qkv_g_p, qkv_g_s = split(qkv_g)
            la_p, la_s = split(la)
            r_g_p, r_g_s = split(r_g)
            zero_state = jnp.zeros((nb, H_GLA * DV_GLA, H_GLA * DK_GLA), F32)
            o_g_p, s_p = _gla(qkv_g_p.reshape(nb, t, -1), la_p.reshape(nb, t, -1), r_g_p.reshape(nb, t, -1),
                              zero_state, gnorm)
            o_g_s, s_s = _gla(qkv_g_s.reshape(db, n_new, -1), la_s.reshape(db, n_new, -1),
                              r_g_s.reshape(db, n_new, -1), _state_to_block_diag(state_gla[j]), gnorm)
            q_d_p, q_d_s = split(q_d)
            kv_d_p, kv_d_s = split(kv_d)
            iq_p, iq_s = split(iq)
            ik2_p, ik2_s = split(ik2)
            misc_p, misc_s = split(misc)
            half = kv_d_p.shape[1] // 2
            o_d_p = _dsa_prompt(rel_bias, q_d_p, iq_p, misc_p.T, kv_d_p[:, :half].astype(BF16),
                                kv_d_p[:, half:].T.astype(BF16), ik2_p.astype(BF16), bias_p, nb, t, tq).T
            pad_lanes = ((0, 0), (0, 0), (0, page - n_new))
            kv_new_t = jnp.pad(jnp.swapaxes(kv_d_s.reshape(db, n_new, -1), 1, 2), pad_lanes).reshape(
                db, 2, half, page)
            ik_new_t = jnp.pad(jnp.swapaxes(ik2_s[:, :D_IDX].reshape(db, n_new, D_IDX), 1, 2), pad_lanes)
            iq_rows = iq_s.reshape(db, n_new * H_IDX, D_IDX)
            iw_rows = misc_s[:, MISC_IW:MISC_IW + H_IDX].reshape(db, n_new * H_IDX, 1)
            o_d_s = _dsa_sample(page_table, q_d_s.reshape(db, n_new, -1), kv_new_t, iq_rows, iw_rows, ik_new_t,
                                bias_s, dsa_cache, dsa_ik_cache_t, j)
            mix_a = jnp.concatenate([o_g_p.reshape(bt, -1), o_g_s.reshape(db * n_new, -1)], axis=0)
            mix_b = jnp.concatenate([o_d_p, o_d_s.reshape(db * n_new, -1)], axis=0)
            outs["gs_p"].append(_block_diag_to_state(s_p))
            outs["gs_s"].append(_block_diag_to_state(s_s))
            outs["dkv_p"].append(kv_d_p.reshape(nb, t, 2, H_DSA, HEAD_DIM))
            outs["dkv_s"].append(kv_d_s.reshape(db, n_new, 2, H_DSA, HEAD_DIM))
            outs["dik_p"].append(ik2_p[:, :D_IDX].reshape(nb, t, D_IDX))
            outs["dik_s"].append(ik2_s[:, :D_IDX].reshape(db, n_new, D_IDX))
        x = _mixout(x, mix_a, mix_b, g, wo[i])
        x = _ffn(x, g, wg[i, 1], wu[i, 1], wd[i, 1], NG_FFN2_PRE, NG_FFN2_POST)
        x = _ple(x, p_all[i], g, wpp[i], wpg[i])

    y_p, y_s = split(x)
    return (y_p.reshape(nb, t, d), y_s.reshape(db, n_new, d),
            jnp.stack(outs["sb_p"]), jnp.stack(outs["sb_s"]),
            jnp.stack(outs["fkv_p"]), jnp.stack(outs["fkv_s"]),
            jnp.stack(outs["flf_p"]), jnp.stack(outs["flf_s"]),
            jnp.stack(outs["gs_p"]), jnp.stack(outs["gs_s"]),
            jnp.stack(outs["dkv_p"]), jnp.stack(outs["dkv_s"]),
            jnp.stack(outs["dik_p"]), jnp.stack(outs["dik_s"]))
```

```python
import functools
import math

import jax
import jax.numpy as jnp
import numpy as np
from jax import lax
from jax.experimental import pallas as pl
from jax.experimental.pallas import tpu as pltpu

F32 = jnp.float32
BF16 = jnp.bfloat16
I32 = jnp.int32

HEAD_DIM = 64
H_SB = 8
H_FOX = 8
H_GLA = 4
DK_GLA = 64
DV_GLA = 128
GLA_RANK = 16
GLA_TAU = 16.0
GLA_CHUNK = 16
H_DSA = 8
H_IDX = 8
D_IDX = 64
DSA_TOPK = 256
N_BUCKETS = 32
MAX_DISTANCE = 128
EPS = 1e-6
MACARON_WEIGHT = 0.5
NG_FFN1_PRE, NG_FFN1_POST, NG_MIX_PRE, NG_MIX_POST, NG_FFN2_PRE, NG_FFN2_POST, NG_PLE_PRE, NG_PLE_POST = range(8)

LANES = 128
VMEM_LIMIT_BYTES = 56 * 1024 * 1024
NEG_BIG = -1e30
INT_MIN = -2 ** 31
LOG2_E = math.log2(math.e)


def _cparams(sem):
    return pltpu.CompilerParams(dimension_semantics=sem, vmem_limit_bytes=VMEM_LIMIT_BYTES)


def _pick_tile(n, cap):
    if n <= cap:
        return n
    for t in range(cap - cap % 8, 7, -8):
        if n % t == 0:
            return t
    return n


def _rms(x, g):
    return x * lax.rsqrt(jnp.mean(x * x, axis=-1, keepdims=True) + EPS) * g


def _log_sigmoid(x):
    return jnp.minimum(x, 0.0) - jnp.log1p(jnp.exp(-jnp.abs(x)))


def _dot(a, b):
    return jnp.dot(a, b, preferred_element_type=F32)


def _dot_nt(a, b):
    return lax.dot_general(a, b, (((1,), (1,)), ((), ())), preferred_element_type=F32)


def _dot_tn(a, b):
    return lax.dot_general(a, b, (((0,), (0,)), ((), ())), preferred_element_type=F32)


def _split2(x):
    hi = x.astype(BF16)
    lo = (x - hi.astype(F32)).astype(BF16)
    return hi, lo


def _split3(x):
    hi = x.astype(BF16)
    r = x - hi.astype(F32)
    mid = r.astype(BF16)
    lo = (r - mid.astype(F32)).astype(BF16)
    return hi, mid, lo


def _iota(shape, dim):
    return lax.broadcasted_iota(I32, shape, dim)


def _ffn_kernel(x_ref, g_ref, wg_ref, wu_ref, wd_ref, o_ref, h_scr, acc_scr, *, pre, post):
    f = pl.program_id(1)

    @pl.when(f == 0)
    def _():
        h_scr[...] = _rms(x_ref[...], g_ref[pre:pre + 1, :]).astype(BF16)
        acc_scr[...] = jnp.zeros_like(acc_scr)

    h = h_scr[...]
    gate = _dot(h, wg_ref[...])
    up = _dot(h, wu_ref[...])
    act = gate * jax.nn.sigmoid(gate) * up
    acc_scr[...] += _dot(act.astype(BF16), wd_ref[...])

    @pl.when(f == pl.num_programs(1) - 1)
    def _():
        o_ref[...] = x_ref[...] + MACARON_WEIGHT * _rms(acc_scr[...], g_ref[post:post + 1, :])


def _ffn(x, g, wg, wu, wd, pre, post):
    m, d = x.shape
    ff = wg.shape[1]
    tm = _pick_tile(m, 512)
    tf = ff
    for cand in range(min(ff, 1408), 127, -128):
        if ff % cand == 0:
            tf = cand
            break
    return pl.pallas_call(
        functools.partial(_ffn_kernel, pre=pre, post=post),
        grid=(m // tm, ff // tf),
        in_specs=[
            pl.BlockSpec((tm, d), lambda i, f: (i, 0)),
            pl.BlockSpec(g.shape, lambda i, f: (0, 0)),
            pl.BlockSpec((d, tf), lambda i, f: (0, f)),
            pl.BlockSpec((d, tf), lambda i, f: (0, f)),
            pl.BlockSpec((tf, d), lambda i, f: (f, 0)),
        ],
        out_specs=pl.BlockSpec((tm, d), lambda i, f: (i, 0)),
        out_shape=jax.ShapeDtypeStruct((m, d), F32),
        scratch_shapes=[pltpu.VMEM((tm, d), BF16), pltpu.VMEM((tm, d), F32)],
        compiler_params=_cparams(("parallel", "arbitrary")),
        name="ffn",
    )(x, g, wg, wu, wd)


AB_Q_SB, AB_KV_SB, AB_Q_FX, AB_KV_FX, AB_F = 0, 512, 1536, 2048, 3072
AB_WIDTH_PAD = 3200


def _proj_ab_kernel(x_ref, g_ref, w_ref, b_ref, qsb_ref, kvsb_ref, qfx_ref, kvfx_ref, lf_ref):
    h = _rms(x_ref[...], g_ref[NG_MIX_PRE:NG_MIX_PRE + 1, :]).astype(BF16)
    qsb_ref[...] = _dot(h, w_ref[:, AB_Q_SB:AB_KV_SB])
    kvsb_ref[...] = _dot(h, w_ref[:, AB_KV_SB:AB_Q_FX])
    qfx_ref[...] = _dot(h, w_ref[:, AB_Q_FX:AB_KV_FX])
    kvfx_ref[...] = _dot(h, w_ref[:, AB_KV_FX:AB_F])
    lf_ref[...] = _log_sigmoid(_dot(h, w_ref[:, AB_F:AB_WIDTH_PAD]) + b_ref[...])


def _proj_ab(x, g, w_pad, b_pad):
    m, d = x.shape
    tm = _pick_tile(m, 512)
    widths = (512, 1024, 512, 1024, LANES)
    return pl.pallas_call(
        _proj_ab_kernel,
        grid=(m // tm,),
        in_specs=[
            pl.BlockSpec((tm, d), lambda i: (i, 0)),
            pl.BlockSpec(g.shape, lambda i: (0, 0)),
            pl.BlockSpec(w_pad.shape, lambda i: (0, 0)),
            pl.BlockSpec(b_pad.shape, lambda i: (0, 0)),
        ],
        out_specs=[pl.BlockSpec((tm, w), lambda i: (i, 0)) for w in widths],
        out_shape=[jax.ShapeDtypeStruct((m, w), F32) for w in widths],
        compiler_params=_cparams(("parallel",)),
        name="proj_ab",
    )(x, g, w_pad, b_pad)


CD_QKV_G, CD_R_G, CD_Q_D, CD_KV_D, CD_IQ, CD_IK2, CD_MISC, CD_WIDTH_PAD = 0, 1024, 1536, 2048, 3072, 3584, 3712, 3840
MISC_IW, MISC_GLR = 0, 8


def _proj_cd_kernel(x_ref, g_ref, w_ref, wgu_ref, bg_ref, qkvg_ref, rg_ref, qd_ref, kvd_ref, iq_ref, ik2_ref,
                    misc_ref, la_ref):
    h = _rms(x_ref[...], g_ref[NG_MIX_PRE:NG_MIX_PRE + 1, :]).astype(BF16)
    qkvg_ref[...] = _dot(h, w_ref[:, CD_QKV_G:CD_R_G])
    rg_ref[...] = _dot(h, w_ref[:, CD_R_G:CD_Q_D])
    qd_ref[...] = _dot(h, w_ref[:, CD_Q_D:CD_KV_D])
    kvd_ref[...] = _dot(h, w_ref[:, CD_KV_D:CD_IQ])
    iq_ref[...] = _dot(h, w_ref[:, CD_IQ:CD_IK2])
    ik2_ref[...] = _dot(h, w_ref[:, CD_IK2:CD_MISC])
    misc = _dot(h, w_ref[:, CD_MISC:CD_WIDTH_PAD])
    misc_ref[...] = misc
    la_ref[...] = _log_sigmoid(_dot(misc.astype(BF16), wgu_ref[...]) + bg_ref[...]) / GLA_TAU


def _proj_cd(x, g, w_pad, wgu_pad, bg):
    m, d = x.shape
    tm = _pick_tile(m, 512)
    widths = (1024, 512, 512, 1024, 512, LANES, LANES, H_GLA * DK_GLA)
    return pl.pallas_call(
        _proj_cd_kernel,
        grid=(m // tm,),
        in_specs=[
            pl.BlockSpec((tm, d), lambda i: (i, 0)),
            pl.BlockSpec(g.shape, lambda i: (0, 0)),
            pl.BlockSpec(w_pad.shape, lambda i: (0, 0)),
            pl.BlockSpec(wgu_pad.shape, lambda i: (0, 0)),
            pl.BlockSpec(bg.shape, lambda i: (0, 0)),
        ],
        out_specs=[pl.BlockSpec((tm, w), lambda i: (i, 0)) for w in widths],
        out_shape=[jax.ShapeDtypeStruct((m, w), F32) for w in widths],
        compiler_params=_cparams(("parallel",)),
        name="proj_cd",
    )(x, g, w_pad, wgu_pad, bg)


def _mixout_kernel(x_ref, a_ref, b_ref, g_ref, wo_ref, o_ref):
    wa = a_ref.shape[1]
    y = _dot(a_ref[...].astype(BF16), wo_ref[:wa, :]) + _dot(b_ref[...].astype(BF16), wo_ref[wa:, :])
    o_ref[...] = x_ref[...] + _rms(y, g_ref[NG_MIX_POST:NG_MIX_POST + 1, :])


def _mixout(x, mix_a, mix_b, g, wo):
    m, d = x.shape
    tm = _pick_tile(m, 512)
    return pl.pallas_call(
        _mixout_kernel,
        grid=(m // tm,),
        in_specs=[
            pl.BlockSpec((tm, d), lambda i: (i, 0)),
            pl.BlockSpec((tm, mix_a.shape[1]), lambda i: (i, 0)),
            pl.BlockSpec((tm, mix_b.shape[1]), lambda i: (i, 0)),
            pl.BlockSpec(g.shape, lambda i: (0, 0)),
            pl.BlockSpec(wo.shape, lambda i: (0, 0)),
        ],
        out_specs=pl.BlockSpec((tm, d), lambda i: (i, 0)),
        out_shape=jax.ShapeDtypeStruct((m, d), F32),
        compiler_params=_cparams(("parallel",)),
        name="mixout",
    )(x, mix_a, mix_b, g, wo)


def _ple_kernel(x_ref, p_ref, g_ref, wpp_ref, wpg_ref, o_ref):
    x = x_ref[...]
    gate = jax.nn.sigmoid(_dot(_rms(x, g_ref[NG_PLE_PRE:NG_PLE_PRE + 1, :]).astype(BF16), wpg_ref[...]))
    proj = _dot(p_ref[...].astype(BF16), wpp_ref[...])
    o_ref[...] = x + _rms(proj * gate, g_ref[NG_PLE_POST:NG_PLE_POST + 1, :])


def _ple(x, p, g, wpp, wpg):
    m, d = x.shape
    tm = _pick_tile(m, 512)
    return pl.pallas_call(
        _ple_kernel,
        grid=(m // tm,),
        in_specs=[
            pl.BlockSpec((tm, d), lambda i: (i, 0)),
            pl.BlockSpec((tm, p.shape[1]), lambda i: (i, 0)),
            pl.BlockSpec(g.shape, lambda i: (0, 0)),
            pl.BlockSpec(wpp.shape, lambda i: (0, 0)),
            pl.BlockSpec(wpg.shape, lambda i: (0, 0)),
        ],
        out_specs=pl.BlockSpec((tm, d), lambda i: (i, 0)),
        out_shape=jax.ShapeDtypeStruct((m, d), F32),
        compiler_params=_cparams(("parallel",)),
        name="ple",
    )(x, p, g, wpp, wpg)


def _pair_masks(q):
    lane = _iota(q.shape, 1)
    return (jnp.where(lane < HEAD_DIM, q, 0.0).astype(BF16), jnp.where(lane >= HEAD_DIM, q, 0.0).astype(BF16))


def _sb_prompt_kernel(q_ref, k_ref, vt_ref, o_ref, acc_scr, c_scr, *, tq):
    i = pl.program_id(1)
    q = q_ref[...] * HEAD_DIM ** -0.5
    qm = []
    for p in range(H_SB // 2):
        qm.extend(_pair_masks(q[:, p * LANES:(p + 1) * LANES]))
    row = _iota((tq, tq), 0)
    col = _iota((tq, tq), 1)
    later_neg = jnp.where(col > row, -1.0, 0.0).astype(BF16)
    acc_scr[...] = jnp.zeros_like(acc_scr)
    c_scr[...] = jnp.zeros_like(c_scr)

    def block(kb, diag):
        ks = pl.multiple_of(kb * tq, tq)
        z2 = []
        for p in range(H_SB // 2):
            kblk = k_ref[pl.ds(ks, tq), p * LANES:(p + 1) * LANES]
            z2.extend(_dot_nt(kblk, qm[2 * p + hh]) * LOG2_E for hh in range(2))
        beta, l2 = [], []
        for zt in z2:
            e = jnp.exp2(-jnp.abs(zt))
            t1 = 1.0 + e
            r = 1.0 / t1
            lt = jnp.maximum(zt, 0.0) + jnp.log2(t1)
            beta.append(jnp.where(zt >= 0.0, r, e * r))
            l2.append(jnp.where(row < col, lt, 0.0) if diag else lt)
        aft = [_dot(later_neg, lt.astype(BF16)) for lt in l2]
        w = []
        for h in range(H_SB):
            c = c_scr[h]
            wt = beta[h] * jnp.exp2(aft[h] + c)
            w.append((jnp.where(row < col, wt, 0.0) if diag else wt).astype(BF16))
            c_scr[h] = c + aft[h][0:1, :] - l2[h][0:1, :]
        for h in range(H_SB):
            acc_scr[h] += _dot(vt_ref[h * HEAD_DIM:(h + 1) * HEAD_DIM, pl.ds(ks, tq)], w[h])

    block(i, True)

    def body(j, carry):
        block(i - 1 - j, False)
        return carry

    lax.fori_loop(0, i, body, 0)
    o_ref[...] = acc_scr[...].reshape(H_SB * HEAD_DIM, tq).T


def _sb_prompt(q, k_bf16, vt_bf16, nb, t):
    tq = _pick_tile(t, 256)
    nq = t // tq
    single = pl.Buffered(1)
    return pl.pallas_call(
        functools.partial(_sb_prompt_kernel, tq=tq),
        grid=(nb, nq),
        in_specs=[
            pl.BlockSpec((tq, q.shape[1]), lambda b, i: (b * nq + i, 0)),
            pl.BlockSpec((t, k_bf16.shape[1]), lambda b, i: (b, 0), pipeline_mode=single),
            pl.BlockSpec((vt_bf16.shape[0], t), lambda b, i: (0, b), pipeline_mode=single),
        ],
        out_specs=pl.BlockSpec((tq, q.shape[1]), lambda b, i: (b * nq + i, 0)),
        out_shape=jax.ShapeDtypeStruct(q.shape, F32),
        scratch_shapes=[pltpu.VMEM((H_SB, HEAD_DIM, tq), F32), pltpu.VMEM((H_SB, 1, tq), F32)],
        compiler_params=_cparams(("parallel", "arbitrary")),
        name="sb_prompt",
    )(q, k_bf16, vt_bf16)


def _cumsum_kernel(x_ref, o_ref, carry_scr, *, tc):
    @pl.when(pl.program_id(1) == 0)
    def _():
        carry_scr[...] = jnp.zeros_like(carry_scr)

    row = _iota((tc, tc), 0)
    col = _iota((tc, tc), 1)
    lower = jnp.where(row >= col, 1.0, 0.0).astype(BF16)
    hi, mid, lo = _split3(x_ref[...])
    cs = _dot(lower, hi) + _dot(lower, mid) + _dot(lower, lo) + carry_scr[...]
    carry_scr[...] = cs[tc - 1:tc, :]
    c_hi = cs.astype(BF16).astype(F32)
    c_mid = (cs - c_hi).astype(BF16).astype(F32)
    c_lo = cs - c_hi - c_mid
    lane = _iota(cs.shape, 1)
    o_ref[...] = jnp.where(lane < H_FOX, c_hi, jnp.where(lane < 2 * H_FOX, c_mid,
                                                          jnp.where(lane < 3 * H_FOX, c_lo, 0.0)))


def _cumsum_rows(x, nb, t):
    tc = _pick_tile(t, 512)
    nt = t // tc
    return pl.pallas_call(
        functools.partial(_cumsum_kernel, tc=tc),
        grid=(nb, nt),
        in_specs=[pl.BlockSpec((tc, LANES), lambda b, j: (b * nt + j, 0))],
        out_specs=pl.BlockSpec((tc, LANES), lambda b, j: (b * nt + j, 0)),
        out_shape=jax.ShapeDtypeStruct((nb * t, LANES), F32),
        scratch_shapes=[pltpu.VMEM((1, LANES), F32)],
        compiler_params=_cparams(("parallel", "arbitrary")),
        name="cumsum_rows",
    )(x)


def _softmax_tiles(logits, values, slots, acc_scr, m_scr, l_scr):
    staged = []
    for s, h in zip(logits, slots):
        m_old = m_scr[h]
        m_new = jnp.maximum(m_old, jnp.max(s, axis=0, keepdims=True))
        alpha = jnp.exp(m_old - m_new)
        pr = jnp.exp(s - m_new)
        l_scr[h] = alpha * l_scr[h] + jnp.sum(pr, axis=0, keepdims=True)
        m_scr[h] = m_new
        staged.append((alpha, pr.astype(BF16)))
    for (alpha, pr), vth, h in zip(staged, values, slots):
        acc_scr[h] = alpha * acc_scr[h] + _dot(vth, pr)


def _fox_prompt_kernel(q_ref, k_ref, vt_ref, cs_ref, o_ref, acc_scr, m_scr, l_scr, *, tq):
    i = pl.program_id(1)
    qs = q_ref[...] * HEAD_DIM ** -0.5
    lane = _iota((tq, LANES), 1)
    qaug = []
    for head in range(H_FOX):
        pick = jnp.where((lane == head) | (lane == head + H_FOX) | (lane == head + 2 * H_FOX), -1.0, 0.0)
        own = (lane < HEAD_DIM) if head % 2 == 0 else (lane >= HEAD_DIM)
        q_pair = qs[:, (head // 2) * LANES:(head // 2 + 1) * LANES]
        qaug.append(jnp.concatenate([jnp.where(own, q_pair, 0.0), pick], axis=1).astype(BF16))
    row = _iota((tq, tq), 0)
    col = _iota((tq, tq), 1)
    acc_scr[...] = jnp.zeros_like(acc_scr)
    m_scr[...] = jnp.full_like(m_scr, NEG_BIG)
    l_scr[...] = jnp.zeros_like(l_scr)

    def block(kb, diag):
        ks = pl.multiple_of(kb * tq, tq)
        cs = cs_ref[pl.ds(ks, tq), :]
        logits = []
        for p in range(H_FOX // 2):
            kaug = jnp.concatenate([k_ref[pl.ds(ks, tq), p * LANES:(p + 1) * LANES], cs], axis=1)
            logits.extend(_dot_nt(kaug, qaug[2 * p + hh]) for hh in range(2))
        if diag:
            logits = [jnp.where(row <= col, s, NEG_BIG) for s in logits]
        values = [vt_ref[h * HEAD_DIM:(h + 1) * HEAD_DIM, pl.ds(ks, tq)] for h in range(H_FOX)]
        _softmax_tiles(logits, values, tuple(range(H_FOX)), acc_scr, m_scr, l_scr)

    block(i, True)

    def body(j, carry):
        block(j, False)
        return carry

    lax.fori_loop(0, i, body, 0)
    o_ref[...] = (acc_scr[...] / l_scr[...]).reshape(H_FOX * HEAD_DIM, tq).T


def _fox_prompt(q, k_bf16, vt_bf16, cum_split_bf16, nb, t):
    tq = _pick_tile(t, 256)
    nq = t // tq
    single = pl.Buffered(1)
    return pl.pallas_call(
        functools.partial(_fox_prompt_kernel, tq=tq),
        grid=(nb, nq),
        in_specs=[
            pl.BlockSpec((tq, q.shape[1]), lambda b, i: (b * nq + i, 0)),
            pl.BlockSpec((t, k_bf16.shape[1]), lambda b, i: (b, 0), pipeline_mode=single),
            pl.BlockSpec((vt_bf16.shape[0], t), lambda b, i: (0, b), pipeline_mode=single),
            pl.BlockSpec((t, LANES), lambda b, i: (b, 0), pipeline_mode=single),
        ],
        out_specs=pl.BlockSpec((tq, q.shape[1]), lambda b, i: (b * nq + i, 0)),
        out_shape=jax.ShapeDtypeStruct(q.shape, F32),
        scratch_shapes=[pltpu.VMEM((H_FOX, HEAD_DIM, tq), F32), pltpu.VMEM((H_FOX, 1, tq), F32),
                        pltpu.VMEM((H_FOX, 1, tq), F32)],
        compiler_params=_cparams(("parallel", "arbitrary")),
        name="fox_prompt",
    )(q, k_bf16, vt_bf16, cum_split_bf16)


def _gla_kernel(qkv_ref, la_ref, r_ref, s0_ref, gn_ref, o_ref, sfin_ref, st_scr, o_scr, e_scr, *, chunk, tt):
    j = pl.program_id(1)
    nk = H_GLA * DK_GLA
    nv = H_GLA * DV_GLA
    c = chunk

    @pl.when(j == 0)
    def _():
        st_scr[...] = s0_ref[0]

    expand = jnp.where(_iota((nk, nv), 0) // DK_GLA == _iota((nk, nv), 1) // DV_GLA, 1.0, 0.0).astype(BF16)
    diag_t = jnp.where(_iota((nv, nk), 0) // DV_GLA == _iota((nv, nk), 1) // DK_GLA, 1.0, 0.0)
    rowc = _iota((c, nk), 0)

    def chunk_body(ci, carry):
        r0 = pl.multiple_of(ci * c, c)
        q = qkv_ref[0, pl.ds(r0, c), 0:nk] * DK_GLA ** -0.5
        k = qkv_ref[0, pl.ds(r0, c), nk:2 * nk]
        v = qkv_ref[0, pl.ds(r0, c), 2 * nk:2 * nk + nv]
        g = la_ref[0, pl.ds(r0, c), :]
        b = jnp.zeros((c, nk), F32)
        for s in range(c):
            b = b + jnp.where(rowc >= s, g[s:s + 1, :], 0.0)
        for s in range(c):
            decay = jnp.where(rowc >= s, jnp.exp(b - b[s:s + 1, :]), 0.0)
            e_scr[s * c:(s + 1) * c, :] = decay * q * k[s:s + 1, :]
        hi, lo = _split2(e_scr[...])
        att = _dot(hi, expand) + _dot(lo, expand)
        st = st_scr[...]
        o = _dot_nt((q * jnp.exp(b)).astype(BF16), st.astype(BF16))
        for s in range(c):
            o = o + att[s * c:(s + 1) * c, :] * v[s:s + 1, :]
        o_scr[pl.ds(r0, c), :] = o
        b_last = b[c - 1:c, :]
        kd = k * jnp.exp(b_last - b)
        st_scr[...] = st * jnp.exp(b_last) + diag_t * _dot_tn(v.astype(BF16), kd.astype(BF16))
        return carry

    lax.fori_loop(0, tt // c, chunk_body, 0)

    o = o_scr[...]
    r = r_ref[0]
    gn = gn_ref[...]
    for h in range(H_GLA):
        sl = slice(h * DV_GLA, (h + 1) * DV_GLA)
        oh = o[:, sl]
        rh = r[:, sl]
        y = oh * lax.rsqrt(jnp.mean(oh * oh, axis=-1, keepdims=True) + EPS) * gn
        o_ref[0, :, sl] = y * (rh * jax.nn.sigmoid(rh))

    @pl.when(j == pl.num_programs(1) - 1)
    def _():
        sfin_ref[0] = st_scr[...]


def _gla(qkv, la, r, s0_t, gnorm):
    nb, t, _ = qkv.shape
    chunk = min(GLA_CHUNK, t)
    tt = _pick_tile(t, 512)
    nk, nv = H_GLA * DK_GLA, H_GLA * DV_GLA
    return pl.pallas_call(
        functools.partial(_gla_kernel, chunk=chunk, tt=tt),
        grid=(nb, t // tt),
        in_specs=[
            pl.BlockSpec((1, tt, qkv.shape[2]), lambda b, j: (b, j, 0)),
            pl.BlockSpec((1, tt, nk), lambda b, j: (b, j, 0)),
            pl.BlockSpec((1, tt, nv), lambda b, j: (b, j, 0)),
            pl.BlockSpec((1, nv, nk), lambda b, j: (b, 0, 0)),
            pl.BlockSpec(gnorm.shape, lambda b, j: (0, 0)),
        ],
        out_specs=[
            pl.BlockSpec((1, tt, nv), lambda b, j: (b, j, 0)),
            pl.BlockSpec((1, nv, nk), lambda b, j: (b, 0, 0)),
        ],
        out_shape=[jax.ShapeDtypeStruct((nb, t, nv), F32), jax.ShapeDtypeStruct(s0_t.shape, F32)],
        scratch_shapes=[pltpu.VMEM((nv, nk), F32), pltpu.VMEM((tt, nv), F32), pltpu.VMEM((chunk * chunk, nk), F32)],
        compiler_params=_cparams(("parallel", "arbitrary")),
        name="gla",
    )(qkv, la, r, s0_t, gnorm)


def _t5_bucket(dist):
    max_exact = N_BUCKETS // 2
    d = jnp.maximum(dist, 0)
    large = max_exact + (jnp.log(jnp.maximum(d, 1).astype(F32) / max_exact)
                         / math.log(MAX_DISTANCE / max_exact) * (N_BUCKETS - max_exact)).astype(I32)
    return jnp.where(d < max_exact, d, jnp.minimum(large, N_BUCKETS - 1))


def _bias_tables_kernel(rb_ref, bp_ref, bs_ref, *, tq, past):
    bucket_p = _t5_bucket(_iota((2 * tq, tq), 1) - _iota((2 * tq, tq), 0) + tq)
    bucket_s = _t5_bucket(past + _iota(bs_ref.shape, 0) // H_DSA - _iota(bs_ref.shape, 1))
    head_s = _iota(bs_ref.shape, 0) % H_DSA
    acc_s = jnp.zeros(bs_ref.shape, F32)
    for h in range(H_DSA):
        acc_p = jnp.zeros((2 * tq, tq), F32)
        for bkt in range(N_BUCKETS):
            val = rb_ref[bkt, h]
            acc_p = jnp.where(bucket_p == bkt, val, acc_p)
            acc_s = jnp.where((bucket_s == bkt) & (head_s == h), val, acc_s)
        bp_ref[h] = acc_p
    bs_ref[...] = acc_s


def _bias_tables(rel_bias, tq, past, n_new):
    cols_s = past + LANES
    return pl.pallas_call(
        functools.partial(_bias_tables_kernel, tq=tq, past=past),
        in_specs=[pl.BlockSpec(memory_space=pltpu.SMEM)],
        out_specs=[pl.BlockSpec(memory_space=pltpu.VMEM), pl.BlockSpec(memory_space=pltpu.VMEM)],
        out_shape=[jax.ShapeDtypeStruct((H_DSA, 2 * tq, tq), F32),
                   jax.ShapeDtypeStruct((n_new * H_DSA, cols_s), F32)],
        compiler_params=pltpu.CompilerParams(vmem_limit_bytes=VMEM_LIMIT_BYTES),
        name="bias_tables",
    )(rel_bias)


def _sort_key(score):
    bits = lax.bitcast_convert_type(score + 0.0, I32)
    return bits ^ ((bits >> 31) & 0x7FFFFFFF)


I16 = jnp.int16
HALF_MIN = -2 ** 15


def _kth_largest_by_halves(key_scr, hi_scr, lo_scr, n_chunks, chunk, kf, n_queries):
    one, zero = jnp.ones((), BF16), jnp.zeros((), BF16)
    groups = chunk // 16

    def count_ge(scr, cand):
        cand16 = cand.astype(I16)

        def body(cb, acc):
            kk = scr[pl.ds(pl.multiple_of(cb * chunk, chunk), chunk), :]
            hit = jnp.where(kk >= cand16, one, zero)
            parts = [hit[g * 16:(g + 1) * 16, :] for g in range(groups)]
            while len(parts) > 1:
                parts = [a + b for a, b in zip(parts[0::2], parts[1::2])]
            return acc + parts[0].astype(F32)

        acc = lax.fori_loop(0, n_chunks, body, jnp.zeros((16, n_queries), F32))
        return jnp.sum(acc, axis=0, keepdims=True)

    def search(scr, want):
        thr0 = jnp.where(count_ge(scr, jnp.zeros((1, n_queries), I32)) >= want, 0, HALF_MIN).astype(I32)

        def bit_body(t, thr):
            cand = thr + (jnp.int32(1) << (14 - t))
            return jnp.where(count_ge(scr, cand) >= want, cand, thr)

        return lax.fori_loop(0, 15, bit_body, thr0)

    thr_hi = search(hi_scr, kf)
    above = count_ge(hi_scr, jnp.minimum(thr_hi + 1, -HALF_MIN - 1))
    above = jnp.where(thr_hi == -HALF_MIN - 1, 0.0, above)

    def fill_lo(cb, carry):
        sl = pl.ds(pl.multiple_of(cb * chunk, chunk), chunk)
        k32 = key_scr[sl, :]
        lo = (k32 & 0xFFFF) + HALF_MIN
        lo_scr[sl, :] = jnp.where((k32 >> 16) == thr_hi, lo, HALF_MIN).astype(I16)
        return carry

    lax.fori_loop(0, n_chunks, fill_lo, 0)
    thr_lo = search(lo_scr, kf - above)
    return thr_hi * 65536 + (thr_lo - HALF_MIN)


def _select_threshold(key_scr, n_chunks, chunk, topk, n_queries, key_axis, halves=None):
    if key_axis == 1:
        chunk_shape, vec_shape = (n_queries, chunk), (n_queries, 1)

        def chunk_slice(cb):
            return (slice(None), pl.ds(pl.multiple_of(cb * chunk, chunk), chunk))

        def count(pred):
            def body(cb, acc):
                idx = cb * chunk + _iota(chunk_shape, 1)
                return acc + jnp.where(pred(key_scr[chunk_slice(cb)], idx), 1.0, 0.0)

            acc = lax.fori_loop(0, n_chunks, body, jnp.zeros(chunk_shape, F32))
            return jnp.sum(acc, axis=1, keepdims=True)
    else:
        chunk_shape, vec_shape = (chunk, n_queries), (1, n_queries)

        def chunk_slice(cb):
            return (pl.ds(pl.multiple_of(cb * chunk, chunk), chunk), slice(None))

        def count(pred):
            def body(cb, acc):
                idx = cb * chunk + _iota(chunk_shape, 0)
                hit = jnp.where(pred(key_scr[chunk_slice(cb)], idx), 1.0, 0.0)
                return acc + jnp.sum(hit.reshape(chunk // 8, 8, n_queries), axis=0)

            acc = lax.fori_loop(0, n_chunks, body, jnp.zeros((8, n_queries), F32))
            return jnp.sum(acc, axis=0, keepdims=True)

    kf = float(topk)
    if halves is None:
        thr0 = jnp.where(count(lambda kk, idx: kk >= 0) >= kf, 0, INT_MIN).astype(I32)

        def bit_body(t, thr):
            cand = thr + (jnp.int32(1) << (30 - t))
            return jnp.where(count(lambda kk, idx: kk >= cand) >= kf, cand, thr)

        thr = lax.fori_loop(0, 31, bit_body, thr0)
    else:
        thr = _kth_largest_by_halves(key_scr, halves[0], halves[1], n_chunks, chunk, kf, n_queries)
    n_gt = count(lambda kk, idx: kk > thr)
    n_eq = count(lambda kk, idx: kk == thr)
    need = kf - n_gt
    trim = (n_eq > need) & (thr > INT_MIN)

    @pl.when(jnp.max(jnp.where(trim, 1.0, 0.0)) > 0.0)
    def _():
        n_bits = max(1, int(math.ceil(math.log2(key_scr.shape[key_axis] + 1))))

        def idx_body(t, cut):
            cand = cut + (jnp.int32(1) << (n_bits - 1 - t))
            return jnp.where(count(lambda kk, idx: (kk == thr) & (idx < cand)) < need, cand, cut)

        cut = lax.fori_loop(0, n_bits, idx_body, jnp.zeros(vec_shape, I32))

        def demote(cb, carry):
            kk = key_scr[chunk_slice(cb)]
            idx = cb * chunk + _iota(chunk_shape, key_axis)
            key_scr[chunk_slice(cb)] = jnp.where(trim & (kk == thr) & (idx > cut), kk - 1, kk)
            return carry

        lax.fori_loop(0, n_chunks, demote, 0)

    return jnp.maximum(thr, INT_MIN + 1)


def _dsa_prompt_kernel(rb_ref, q_ref, iq_ref, iwt_ref, k_ref, vt_ref, ik_ref, bias_ref, o_ref,
                       key_scr, khi_scr, klo_scr, acc_scr, m_scr, l_scr, *, tq, topk):
    i = pl.program_id(1)
    npair = H_DSA // 2
    row = _iota((tq, tq), 0)
    col = _iota((tq, tq), 1)

    iq = iq_ref[...]
    iqm = []
    for p in range(H_IDX // 2):
        iqm.extend(_pair_masks(iq[:, p * LANES:(p + 1) * LANES]))
    wq = [iwt_ref[MISC_IW + h:MISC_IW + h + 1, :] * H_IDX ** -0.5 for h in range(H_IDX)]

    def score_block(kb, diag):
        ks = pl.multiple_of(kb * tq, tq)
        ik2 = ik_ref[pl.ds(ks, tq), :]
        score = jnp.zeros((tq, tq), F32)
        for h in range(H_IDX):
            score = score + jnp.maximum(_dot_nt(ik2, iqm[h]), 0.0) * wq[h]
        key = _sort_key(score)
        if diag:
            key = jnp.where(row <= col, key, INT_MIN)
        key_scr[pl.ds(ks, tq), :] = key
        khi_scr[pl.ds(ks, tq), :] = (key >> 16).astype(I16)

    score_block(i, True)

    def score_body(kb, carry):
        score_block(kb, False)
        return carry

    lax.fori_loop(0, i, score_body, 0)

    thr = _select_threshold(key_scr, i + 1, tq, topk, tq, key_axis=0, halves=(khi_scr, klo_scr))

    q = q_ref[...] * HEAD_DIM ** -0.5
    qm = []
    for p in range(npair):
        qm.extend(_pair_masks(q[:, p * LANES:(p + 1) * LANES]))
    acc_scr[...] = jnp.zeros_like(acc_scr)
    m_scr[...] = jnp.full_like(m_scr, NEG_BIG)
    l_scr[...] = jnp.zeros_like(l_scr)

    def attend_block(kb, near):
        ks = pl.multiple_of(kb * tq, tq)
        sel = key_scr[pl.ds(ks, tq), :] >= thr
        logits, values = [], []
        for p in range(npair):
            kblk = k_ref[pl.ds(ks, tq), p * LANES:(p + 1) * LANES]
            for hh in range(2):
                h = 2 * p + hh
                if near == 0:
                    bias = rb_ref[N_BUCKETS - 1, h]
                else:
                    bias = bias_ref[h, (near - 1) * tq:near * tq, :]
                logits.append(jnp.where(sel, _dot_nt(kblk, qm[h]) + bias, NEG_BIG))
                values.append(vt_ref[h * HEAD_DIM:(h + 1) * HEAD_DIM, pl.ds(ks, tq)])
        _softmax_tiles(logits, values, tuple(range(H_DSA)), acc_scr, m_scr, l_scr)

    attend_block(i, 2)

    @pl.when(i >= 1)
    def _():
        attend_block(i - 1, 1)

    def attend_body(kb, carry):
        attend_block(kb, 0)
        return carry

    lax.fori_loop(0, jnp.maximum(i - 1, 0), attend_body, 0)
    o_ref[...] = (acc_scr[...] / l_scr[...]).reshape(H_DSA * HEAD_DIM, tq).T


def _dsa_prompt(rel_bias, q, iq, iw_t, k_bf16, vt_bf16, ik2_bf16, bias_p, nb, t, tq):
    nq = t // tq
    topk = min(DSA_TOPK, t // 4)
    single = pl.Buffered(1)
    return pl.pallas_call(
        functools.partial(_dsa_prompt_kernel, tq=tq, topk=topk),
        grid=(nb, nq),
        in_specs=[
            pl.BlockSpec(memory_space=pltpu.SMEM),
            pl.BlockSpec((tq, q.shape[1]), lambda b, i: (b * nq + i, 0)),
            pl.BlockSpec((tq, iq.shape[1]), lambda b, i: (b * nq + i, 0)),
            pl.BlockSpec((LANES, tq), lambda b, i: (0, b * nq + i)),
            pl.BlockSpec((t, k_bf16.shape[1]), lambda b, i: (b, 0), pipeline_mode=single),
            pl.BlockSpec((vt_bf16.shape[0], t), lambda b, i: (0, b), pipeline_mode=single),
            pl.BlockSpec((t, LANES), lambda b, i: (b, 0), pipeline_mode=single),
            pl.BlockSpec(bias_p.shape, lambda b, i: (0, 0, 0), pipeline_mode=single),
        ],
        out_specs=pl.BlockSpec((tq, q.shape[1]), lambda b, i: (b * nq + i, 0)),
        out_shape=jax.ShapeDtypeStruct(q.shape, F32),
        scratch_shapes=[pltpu.VMEM((t, tq), I32), pltpu.VMEM((t, tq), I16), pltpu.VMEM((t, tq), I16),
                        pltpu.VMEM((H_DSA, HEAD_DIM, tq), F32),
                        pltpu.VMEM((H_DSA, 1, tq), F32), pltpu.VMEM((H_DSA, 1, tq), F32)],
        compiler_params=_cparams(("parallel", "arbitrary")),
        name="dsa_prompt",
    )(rel_bias, q, iq, iw_t, k_bf16, vt_bf16, ik2_bf16, bias_p)


def _rows_by_head(x, n_heads):
    t, w = x.shape
    head_mask = _iota((n_heads, w), 1) // HEAD_DIM == _iota((n_heads, w), 0)
    return jnp.concatenate([jnp.where(head_mask, x[i:i + 1, :], 0.0) for i in range(t)], axis=0)


def _collapse_heads(acc, n_new, n_heads):
    w = acc.shape[1]
    head_mask = _iota((n_heads, w), 1) // HEAD_DIM == _iota((n_heads, w), 0)
    rows = [jnp.sum(jnp.where(head_mask, acc[i * n_heads:(i + 1) * n_heads, :], 0.0), axis=0, keepdims=True)
            for i in range(n_new)]
    return jnp.concatenate(rows, axis=0)


def _sb_sample_kernel(pt_ref, q_ref, kvn_ref, *refs, n_pages, n_new):
    del pt_ref
    kt_pages = refs[:n_pages]
    vt_pages = refs[n_pages:2 * n_pages]
    o_ref = refs[2 * n_pages + 1]
    hw = H_SB * HEAD_DIM
    qbd = _rows_by_head(q_ref[0] * HEAD_DIM ** -0.5, H_SB)
    qbd_b = qbd.astype(BF16)
    nr = n_new * H_SB
    trow = _iota((nr, 1), 0) // H_SB
    kn = kvn_ref[0, :, :hw]
    vn = kvn_ref[0, :, hw:]
    lg_new, z_new = [], []
    for j in range(n_new):
        z = jnp.sum(qbd * kn[j:j + 1, :], axis=1, keepdims=True)
        z_new.append(z)
        lg_new.append(jnp.where(j < trow, _log_sigmoid(-z), 0.0))
    acc = jnp.zeros((nr, hw), F32)
    carry = jnp.zeros((nr, 1), F32)
    for j in reversed(range(n_new)):
        w = jnp.where(j < trow, jnp.exp(z_new[j] + lg_new[j] + carry), 0.0)
        acc = acc + w * vn[j:j + 1, :]
        carry = carry + lg_new[j]
    upper = jnp.where(_iota((LANES, LANES), 0) > _iota((LANES, LANES), 1), 1.0, 0.0).astype(BF16)
    z = [_dot(qbd_b, kt_pages[pg][0, 0, 0].astype(BF16)) for pg in range(n_pages)]
    lg = [_log_sigmoid(-zp) for zp in z]
    aft = []
    for lgp in lg:
        hi, lo = _split2(lgp)
        aft.append(_dot(hi, upper) + _dot(lo, upper))
    for pg in reversed(range(n_pages)):
        w = jnp.exp(z[pg] + lg[pg] + aft[pg] + carry)
        acc = acc + _dot_nt(w.astype(BF16), vt_pages[pg][0, 0, 0].astype(BF16))
        carry = carry + aft[pg][:, 0:1] + lg[pg][:, 0:1]
    o_ref[0] = _collapse_heads(acc, n_new, H_SB)


def _page_specs(n_pages, layer, block, tail):
    def spec(j):
        return pl.BlockSpec(block, lambda b, pt: (layer, pt[b, j]) + tail)
    return [spec(j) for j in range(n_pages)]


def _kv_page_specs(n_pages, layer, cache_t):
    block = (1, 1, 1) + cache_t.shape[3:]
    return _page_specs(n_pages, layer, block, (0, 0, 0)) + _page_specs(n_pages, layer, block, (1, 0, 0))


def _seq_spec(n_new, width, seq0):
    return pl.BlockSpec((1, n_new, width), lambda b, pt: (seq0 + b, 0, 0))


def _sb_sample(page_table, q, kv_new, cache_t, layer, mix, seq0):
    _, n_new, hw = q.shape
    db, n_pages = page_table.shape
    grid_spec = pltpu.PrefetchScalarGridSpec(
        num_scalar_prefetch=1,
        grid=(db,),
        in_specs=[_seq_spec(n_new, hw, seq0), _seq_spec(n_new, 2 * hw, seq0)]
        + _kv_page_specs(n_pages, layer, cache_t) + [pl.BlockSpec(memory_space=pl.ANY)],
        out_specs=_seq_spec(n_new, hw, seq0),
    )
    return pl.pallas_call(
        functools.partial(_sb_sample_kernel, n_pages=n_pages, n_new=n_new),
        grid_spec=grid_spec,
        out_shape=jax.ShapeDtypeStruct(mix.shape, F32),
        input_output_aliases={3 + 2 * n_pages: 0},
        compiler_params=_cparams(("parallel",)),
        name="sb_sample",
    )(page_table, q, kv_new, *([cache_t] * (2 * n_pages)), mix)


def _head_column(x, n_new, n_heads, pick):
    lane_is_head = _iota((n_heads, x.shape[1]), 1) == _iota((n_heads, x.shape[1]), 0)
    cols = [jnp.sum(jnp.where(lane_is_head, x[pick(i):pick(i) + 1, :], 0.0), axis=1, keepdims=True)
            for i in range(n_new)]
    return jnp.concatenate(cols, axis=0)


def _fox_sample_kernel(pt_ref, q_ref, kvn_ref, lfn_ref, *refs, n_pages, n_new):
    del pt_ref
    kt_pages = refs[:n_pages]
    vt_pages = refs[n_pages:2 * n_pages]
    lf_pages = refs[2 * n_pages:3 * n_pages]
    o_ref = refs[3 * n_pages + 1]
    hw = H_FOX * HEAD_DIM
    qbd = _rows_by_head(q_ref[0] * HEAD_DIM ** -0.5, H_FOX)
    qbd_b = qbd.astype(BF16)
    nr = n_new * H_FOX
    trow = _iota((nr, 1), 0) // H_FOX
    kn = kvn_ref[0, :, :hw]
    vn = kvn_ref[0, :, hw:]
    lfn = lfn_ref[0]
    csum = [lfn[0:1, :]]
    for j in range(1, n_new):
        csum.append(csum[-1] + lfn[j:j + 1, :])
    csum = jnp.concatenate(csum, axis=0)
    c_query = _head_column(csum, n_new, H_FOX, lambda t: t)
    s_new = []
    for j in range(n_new):
        c_key = _head_column(csum, n_new, H_FOX, lambda t, j=j: j)
        z = jnp.sum(qbd * kn[j:j + 1, :], axis=1, keepdims=True)
        s_new.append(jnp.where(j <= trow, z + (c_query - c_key), NEG_BIG))
    upper = jnp.where(_iota((LANES, LANES), 0) > _iota((LANES, LANES), 1), 1.0, 0.0).astype(BF16)
    z = [_dot(qbd_b, kt_pages[pg][0, 0, 0].astype(BF16)) for pg in range(n_pages)]
    lf, aft = [], []
    for pg in range(n_pages):
        lfp = jnp.concatenate([lf_pages[pg][0, 0]] * n_new, axis=0)
        hi, mid, lo = _split3(lfp)
        lf.append(lfp)
        aft.append(_dot(hi, upper) + _dot(mid, upper) + _dot(lo, upper))
    carry = c_query
    s = [None] * n_pages
    for pg in reversed(range(n_pages)):
        s[pg] = z[pg] + aft[pg] + carry
        carry = carry + aft[pg][:, 0:1] + lf[pg][:, 0:1]
    m = s_new[0]
    for j in range(1, n_new):
        m = jnp.maximum(m, s_new[j])
    for pg in range(n_pages):
        m = jnp.maximum(m, jnp.max(s[pg], axis=1, keepdims=True))
    l = jnp.zeros((nr, 1), F32)
    acc = jnp.zeros((nr, hw), F32)
    for j in range(n_new):
        pr = jnp.exp(s_new[j] - m)
        l = l + pr
        acc = acc + pr * vn[j:j + 1, :]
    for pg in range(n_pages):
        pr = jnp.exp(s[pg] - m)
        l = l + jnp.sum(pr, axis=1, keepdims=True)
        acc = acc + _dot_nt(pr.astype(BF16), vt_pages[pg][0, 0, 0].astype(BF16))
    o_ref[0] = _collapse_heads(acc / l, n_new, H_FOX)


def _fox_sample(page_table, q, kv_new, lf_new, cache_t, cache_lf_t, layer, mix, seq0):
    _, n_new, hw = q.shape
    db, n_pages = page_table.shape
    page = cache_lf_t.shape[3]
    grid_spec = pltpu.PrefetchScalarGridSpec(
        num_scalar_prefetch=1,
        grid=(db,),
        in_specs=[_seq_spec(n_new, hw, seq0), _seq_spec(n_new, 2 * hw, seq0), _seq_spec(n_new, LANES, seq0)]
        + _kv_page_specs(n_pages, layer, cache_t)
        + _page_specs(n_pages, layer, (1, 1, H_FOX, page), (0, 0)) + [pl.BlockSpec(memory_space=pl.ANY)],
        out_specs=_seq_spec(n_new, hw, seq0),
    )
    return pl.pallas_call(
        functools.partial(_fox_sample_kernel, n_pages=n_pages, n_new=n_new),
        grid_spec=grid_spec,
        out_shape=jax.ShapeDtypeStruct(mix.shape, F32),
        input_output_aliases={4 + 3 * n_pages: 0},
        compiler_params=_cparams(("parallel",)),
        name="fox_sample",
    )(page_table, q, kv_new, lf_new, *([cache_t] * (2 * n_pages)), *([cache_lf_t] * n_pages), mix)


def _dsa_sample_kernel(pt_ref, q_ref, kvn_ref, iq_ref, iw_ref, ikn_ref, bias_ref, *refs, n_pages, n_new, topk):
    del pt_ref
    kt_pages = refs[:n_pages]
    vt_pages = refs[n_pages:2 * n_pages]
    ik_pages = refs[2 * n_pages:3 * n_pages]
    o_ref = refs[3 * n_pages + 1]
    key_scr = refs[3 * n_pages + 2]
    hw = H_DSA * HEAD_DIM
    nr = n_new * H_DSA
    page = ik_pages[0].shape[3]
    past = n_pages * page
    trow = _iota((nr, LANES), 0) // H_DSA
    lane = _iota((nr, LANES), 1)

    iq = iq_ref[0].astype(BF16)
    iw = iw_ref[0] * H_IDX ** -0.5

    def head_sum(x):
        cols = x.shape[1]
        g = jnp.sum(x.reshape(n_new, H_IDX, cols), axis=1, keepdims=True)
        return jnp.broadcast_to(g, (n_new, H_IDX, cols)).reshape(nr, cols)

    for pg in range(n_pages):
        ikp = ik_pages[pg][0, 0].astype(BF16)
        sc = head_sum(jnp.maximum(_dot(iq, ikp), 0.0) * iw)
        key_scr[:, pg * page:(pg + 1) * page] = _sort_key(sc)
    ikn = ikn_ref[0].astype(BF16)
    sc = head_sum(jnp.maximum(_dot(iq, ikn), 0.0) * iw)
    key_scr[:, past:past + LANES] = jnp.where(lane <= trow, _sort_key(sc), INT_MIN)

    thr = _select_threshold(key_scr, 1, past + LANES, topk, nr, key_axis=1)

    qbd_b = _rows_by_head(q_ref[0] * HEAD_DIM ** -0.5, H_DSA).astype(BF16)
    def keys_t(pg):
        return (kt_pages[pg][0, 0, 0] if pg < n_pages else kvn_ref[0, 0]).astype(BF16)

    def values_t(pg):
        return (vt_pages[pg][0, 0, 0] if pg < n_pages else kvn_ref[0, 1]).astype(BF16)

    s = []
    for pg in range(n_pages + 1):
        sel = key_scr[:, pg * page:(pg + 1) * page] >= thr
        s.append(jnp.where(sel, _dot(qbd_b, keys_t(pg)) + bias_ref[:, pg * page:(pg + 1) * page], NEG_BIG))
    m = jnp.max(s[0], axis=1, keepdims=True)
    for pg in range(1, n_pages + 1):
        m = jnp.maximum(m, jnp.max(s[pg], axis=1, keepdims=True))
    l = jnp.zeros((nr, 1), F32)
    acc = jnp.zeros((nr, hw), F32)
    for pg in range(n_pages + 1):
        pr = jnp.exp(s[pg] - m)
        l = l + jnp.sum(pr, axis=1, keepdims=True)
        acc = acc + _dot_nt(pr.astype(BF16), values_t(pg))
    o_ref[0] = _collapse_heads(acc / l, n_new, H_DSA)


def _dsa_sample(page_table, q, kv_new_t, iq_rows, iw_rows, ik_new_t, bias_s, cache_t, cache_ik_t, layer, mix, seq0):
    _, n_new, hw = q.shape
    db, n_pages = page_table.shape
    page = cache_ik_t.shape[3]
    nr = n_new * H_DSA
    topk = min(DSA_TOPK, (n_pages * page + n_new) // 4)
    grid_spec = pltpu.PrefetchScalarGridSpec(
        num_scalar_prefetch=1,
        grid=(db,),
        in_specs=[_seq_spec(n_new, hw, seq0),
                  pl.BlockSpec((1, 2, hw, page), lambda b, pt: (b, 0, 0, 0)),
                  _seq_spec(nr, D_IDX, seq0),
                  pl.BlockSpec((1, nr, 1), lambda b, pt: (b, 0, 0)),
                  pl.BlockSpec((1, D_IDX, page), lambda b, pt: (b, 0, 0)),
                  pl.BlockSpec(bias_s.shape, lambda b, pt: (0, 0))]
        + _kv_page_specs(n_pages, layer, cache_t)
        + _page_specs(n_pages, layer, (1, 1, D_IDX, page), (0, 0)) + [pl.BlockSpec(memory_space=pl.ANY)],
        out_specs=_seq_spec(n_new, hw, seq0),
        scratch_shapes=[pltpu.VMEM((nr, n_pages * page + LANES), I32)],
    )
    return pl.pallas_call(
        functools.partial(_dsa_sample_kernel, n_pages=n_pages, n_new=n_new, topk=topk),
        grid_spec=grid_spec,
        out_shape=jax.ShapeDtypeStruct(mix.shape, F32),
        input_output_aliases={7 + 3 * n_pages: 0},
        compiler_params=_cparams(("parallel",)),
        name="dsa_sample",
    )(page_table, q, kv_new_t, iq_rows, iw_rows, ik_new_t, bias_s, *([cache_t] * (2 * n_pages)),
      *([cache_ik_t] * n_pages), mix)


def _state_to_block_diag(s):
    nb = s.shape[0]
    eye = jnp.eye(H_GLA, dtype=s.dtype)
    return jnp.einsum("bhkv,hg->bhvgk", s, eye).reshape(nb, H_GLA * DV_GLA, H_GLA * DK_GLA)


def _block_diag_to_state(bd):
    nb = bd.shape[0]
    bd = bd.reshape(nb, H_GLA, DV_GLA, H_GLA, DK_GLA)
    return jnp.stack([jnp.swapaxes(bd[:, h, :, h, :], 1, 2) for h in range(H_GLA)], axis=1)


def _pad_cols(w, width):
    return jnp.pad(w, ((0, 0), (0, width - w.shape[1])))


def _cd_weight_layout(w):
    gq = H_GLA * DK_GLA
    gv = H_GLA * DV_GLA
    dh = H_DSA * HEAD_DIM
    offs = np.cumsum([0, gq, gq, gv, GLA_RANK, gv, dh, dh, dh, H_IDX * D_IDX, D_IDX, H_IDX])
    seg = [w[:, offs[i]:offs[i + 1]] for i in range(11)]
    q_g, k_g, v_g, g_lr, r_g, q_d, k_d, v_d, iq, ik, iw = seg
    misc = _pad_cols(jnp.concatenate([iw, g_lr], axis=1), LANES)
    return jnp.concatenate([q_g, k_g, v_g, r_g, q_d, k_d, v_d, iq, ik, ik, misc], axis=1)


def kernel(x_prompt, x_sample, p_prompt, p_sample, cache_sb_kv, cache_fox_kv, cache_fox_logf, state_gla,
           cache_dsa_kv, cache_dsa_idxk, page_table, norm_gains, ffn_w_gate, ffn_w_up, ffn_w_down, w_out,
           ple_w_proj, ple_w_gate, ab_w_in, ab_b_forget, cd_w_in, gla_w_gate_up, gla_b_gate, gla_norm, rel_bias):
    nb, t, d = x_prompt.shape
    db, n_new, _ = x_sample.shape
    depth = norm_gains.shape[0]
    bt = nb * t
    n_pages = page_table.shape[1]
    page = cache_sb_kv.shape[2]
    past = n_pages * page
    tq = _pick_tile(t, 256)

    x = jnp.concatenate([x_prompt.reshape(bt, d), x_sample.reshape(db * n_new, d)], axis=0)
    p_all = jnp.concatenate([p_prompt.reshape(depth, bt, -1), p_sample.reshape(depth, db * n_new, -1)], axis=1)

    wg = ffn_w_gate.astype(BF16)
    wu = ffn_w_up.astype(BF16)
    wd = ffn_w_down.astype(BF16)
    wo = w_out.astype(BF16)
    wpp = ple_w_proj.astype(BF16)
    wpg = ple_w_gate.astype(BF16)

    def pages_last(c):
        ct = jnp.transpose(c, (0, 1, 3, 4, 5, 2))
        return ct.reshape(ct.shape[:3] + (-1, ct.shape[5]))

    sb_cache = pages_last(cache_sb_kv)
    fox_cache = pages_last(cache_fox_kv)
    dsa_cache = pages_last(cache_dsa_kv)
    fox_lf_cache_t = jnp.swapaxes(cache_fox_logf, 2, 3)
    dsa_ik_cache_t = jnp.swapaxes(cache_dsa_idxk, 2, 3)

    bias_p, bias_s = _bias_tables(rel_bias, tq, past, n_new)

    def split(a):
        return a[:bt], a[bt:]

    seq0 = bt // n_new

    def by_seq(a):
        return a.reshape(-1, n_new, a.shape[1])

    outs = {k: [] for k in ("sb_p", "sb_s", "fkv_p", "fkv_s", "flf_p", "flf_s", "gs_p", "gs_s", "dkv_p", "dkv_s",
                            "dik_p", "dik_s")}
    for i in range(depth):
        j = i // 2
        g = norm_gains[i]
        x = _ffn(x, g, wg[i, 0], wu[i, 0], wd[i, 0], NG_FFN1_PRE, NG_FFN1_POST)
        if i % 2 == 0:
            w_f = ab_w_in[j][:, AB_F:]
            w_pad = _pad_cols(jnp.concatenate([ab_w_in[j], w_f, w_f], axis=1), AB_WIDTH_PAD).astype(BF16)
            b_pad = _pad_cols(jnp.tile(ab_b_forget[j], 3)[None, :], LANES)
            q_sb, kv_sb, q_fx, kv_fx, lf = _proj_ab(x, g, w_pad, b_pad)
            kv_sb_p, kv_sb_s = split(kv_sb)
            kv_fx_p, kv_fx_s = split(kv_fx)
            lf_p, lf_s = split(lf)
            cum_split = _cumsum_rows(lf, nb, t).astype(BF16)
            half = kv_sb.shape[1] // 2
            mix_a = _sb_prompt(q_sb, kv_sb[:, :half].astype(BF16), kv_sb[:, half:].T.astype(BF16), nb, t)
            mix_b = _fox_prompt(q_fx, kv_fx[:, :half].astype(BF16), kv_fx[:, half:].T.astype(BF16), cum_split,
                                nb, t)
            mix_a = _sb_sample(page_table, by_seq(q_sb), by_seq(kv_sb), sb_cache, j, by_seq(mix_a),
                               seq0).reshape(mix_a.shape)
            mix_b = _fox_sample(page_table, by_seq(q_fx), by_seq(kv_fx), by_seq(lf), fox_cache, fox_lf_cache_t, j,
                                by_seq(mix_b), seq0).reshape(mix_b.shape)
            outs["sb_p"].append(kv_sb_p.reshape(nb, t, 2, H_SB, HEAD_DIM))
            outs["sb_s"].append(kv_sb_s.reshape(db, n_new, 2, H_SB, HEAD_DIM))
            outs["fkv_p"].append(kv_fx_p.reshape(nb, t, 2, H_FOX, HEAD_DIM))
            outs["fkv_s"].append(kv_fx_s.reshape(db, n_new, 2, H_FOX, HEAD_DIM))
            outs["flf_p"].append(lf_p[:, :H_FOX].reshape(nb, t, H_FOX))
            outs["flf_s"].append(lf_s[:, :H_FOX].reshape(db, n_new, H_FOX))
        else:
            w_pad = _pad_cols(_cd_weight_layout(cd_w_in[j]), CD_WIDTH_PAD).astype(BF16)
            wgu_pad = jnp.zeros((LANES, H_GLA * DK_GLA), F32).at[MISC_GLR:MISC_GLR + GLA_RANK].set(
                gla_w_gate_up[j]).astype(BF16)
            qkv_g, r_g, q_d, kv_d, iq, ik2, misc, la = _proj_cd(x, g, w_pad, wgu_pad, gla_b_gate[j][None, :])
            gnorm = gla_norm[j][None, :]
            qkv_g_p, qkv_g_s = split(qkv_g)
            la_p, la_s = split(la)
            r_g_p, r_g_s = split(r_g)
            zero_state = jnp.zeros((nb, H_GLA * DV_GLA, H_GLA * DK_GLA), F32)
            o_g_p, s_p = _gla(qkv_g_p.reshape(nb, t, -1), la_p.reshape(nb, t, -1), r_g_p.reshape(nb, t, -1),
                              zero_state, gnorm)
            o_g_s, s_s = _gla(qkv_g_s.reshape(db, n_new, -1), la_s.reshape(db, n_new, -1),
                              r_g_s.reshape(db, n_new, -1), _state_to_block_diag(state_gla[j]), gnorm)
            kv_d_p, kv_d_s = split(kv_d)
            ik2_p, ik2_s = split(ik2)
            misc_s = misc[bt:]
            half = kv_d.shape[1] // 2
            mix_b = _dsa_prompt(rel_bias, q_d, iq, misc.T, kv_d[:, :half].astype(BF16),
                                kv_d[:, half:].T.astype(BF16), ik2.astype(BF16), bias_p, nb, t, tq)
            pad_lanes = ((0, 0), (0, 0), (0, page - n_new))
            kv_new_t = jnp.pad(jnp.swapaxes(kv_d_s.reshape(db, n_new, -1), 1, 2), pad_lanes).reshape(
                db, 2, half, page)
            ik_new_t = jnp.pad(jnp.swapaxes(ik2_s[:, :D_IDX].reshape(db, n_new, D_IDX), 1, 2), pad_lanes)
            iq_rows = iq.reshape(-1, n_new * H_IDX, D_IDX)
            iw_rows = misc_s[:, MISC_IW:MISC_IW + H_IDX].reshape(db, n_new * H_IDX, 1)
            mix_b = _dsa_sample(page_table, by_seq(q_d), kv_new_t, iq_rows, iw_rows, ik_new_t, bias_s, dsa_cache,
                                dsa_ik_cache_t, j, by_seq(mix_b), seq0).reshape(mix_b.shape)
            mix_a = jnp.concatenate([o_g_p.reshape(bt, -1), o_g_s.reshape(db * n_new, -1)], axis=0)
            outs["gs_p"].append(_block_diag_to_state(s_p))
            outs["gs_s"].append(_block_diag_to_state(s_s))
            outs["dkv_p"].append(kv_d_p.reshape(nb, t, 2, H_DSA, HEAD_DIM))
            outs["dkv_s"].append(kv_d_s.reshape(db, n_new, 2, H_DSA, HEAD_DIM))
            outs["dik_p"].append(ik2_p[:, :D_IDX].reshape(nb, t, D_IDX))
            outs["dik_s"].append(ik2_s[:, :D_IDX].reshape(db, n_new, D_IDX))
        x = _mixout(x, mix_a, mix_b, g, wo[i])
        x = _ffn(x, g, wg[i, 1], wu[i, 1], wd[i, 1], NG_FFN2_PRE, NG_FFN2_POST)
        x = _ple(x, p_all[i], g, wpp[i], wpg[i])

    y_p, y_s = split(x)
    return (y_p.reshape(nb, t, d), y_s.reshape(db, n_new, d),
            jnp.stack(outs["sb_p"]), jnp.stack(outs["sb_s"]),
            jnp.stack(outs["fkv_p"]), jnp.stack(outs["fkv_s"]),
            jnp.stack(outs["flf_p"]), jnp.stack(outs["flf_s"]),
            jnp.stack(outs["gs_p"]), jnp.stack(outs["gs_s"]),
            jnp.stack(outs["dkv_p"]), jnp.stack(outs["dkv_s"]),
            jnp.stack(outs["dik_p"]), jnp.stack(outs["dik_s"]))
```

```python
import functools
import math

import jax
import jax.numpy as jnp
import numpy as np
from jax import lax
from jax.experimental import pallas as pl
from jax.experimental.pallas import tpu as pltpu

F32 = jnp.float32
BF16 = jnp.bfloat16
I32 = jnp.int32

HEAD_DIM = 64
H_SB = 8
H_FOX = 8
H_GLA = 4
DK_GLA = 64
DV_GLA = 128
GLA_RANK = 16
GLA_TAU = 16.0
GLA_CHUNK = 16
H_DSA = 8
H_IDX = 8
D_IDX = 64
DSA_TOPK = 256
N_BUCKETS = 32
MAX_DISTANCE = 128
EPS = 1e-6
MACARON_WEIGHT = 0.5
NG_FFN1_PRE, NG_FFN1_POST, NG_MIX_PRE, NG_MIX_POST, NG_FFN2_PRE, NG_FFN2_POST, NG_PLE_PRE, NG_PLE_POST = range(8)

LANES = 128
VMEM_LIMIT_BYTES = 56 * 1024 * 1024
NEG_BIG = -1e30
INT_MIN = -2 ** 31
LOG2_E = math.log2(math.e)


def _cparams(sem):
    return pltpu.CompilerParams(dimension_semantics=sem, vmem_limit_bytes=VMEM_LIMIT_BYTES)


def _pick_tile(n, cap):
    if n <= cap:
        return n
    for t in range(cap - cap % 8, 7, -8):
        if n % t == 0:
            return t
    return n


def _rms(x, g):
    return x * lax.rsqrt(jnp.mean(x * x, axis=-1, keepdims=True) + EPS) * g


def _log_sigmoid(x):
    return jnp.minimum(x, 0.0) - jnp.log1p(jnp.exp(-jnp.abs(x)))


def _dot(a, b):
    return jnp.dot(a, b, preferred_element_type=F32)


def _dot_nt(a, b):
    return lax.dot_general(a, b, (((1,), (1,)), ((), ())), preferred_element_type=F32)


def _dot_tn(a, b):
    return lax.dot_general(a, b, (((0,), (0,)), ((), ())), preferred_element_type=F32)


def _split2(x):
    hi = x.astype(BF16)
    lo = (x - hi.astype(F32)).astype(BF16)
    return hi, lo


def _split3(x):
    hi = x.astype(BF16)
    r = x - hi.astype(F32)
    mid = r.astype(BF16)
    lo = (r - mid.astype(F32)).astype(BF16)
    return hi, mid, lo


def _iota(shape, dim):
    return lax.broadcasted_iota(I32, shape, dim)


def _ffn_kernel(x_ref, g_ref, wg_ref, wu_ref, wd_ref, o_ref, h_scr, acc_scr, *, pre, post):
    f = pl.program_id(1)

    @pl.when(f == 0)
    def _():
        h_scr[...] = _rms(x_ref[...], g_ref[pre:pre + 1, :]).astype(BF16)
        acc_scr[...] = jnp.zeros_like(acc_scr)

    h = h_scr[...]
    gate = _dot(h, wg_ref[...])
    up = _dot(h, wu_ref[...])
    act = gate * jax.nn.sigmoid(gate) * up
    acc_scr[...] += _dot(act.astype(BF16), wd_ref[...])

    @pl.when(f == pl.num_programs(1) - 1)
    def _():
        o_ref[...] = x_ref[...] + MACARON_WEIGHT * _rms(acc_scr[...], g_ref[post:post + 1, :])


def _ffn(x, g, wg, wu, wd, pre, post):
    m, d = x.shape
    ff = wg.shape[1]
    tm = _pick_tile(m, 512)
    tf = ff
    for cand in range(min(ff, 1408), 127, -128):
        if ff % cand == 0:
            tf = cand
            break
    return pl.pallas_call(
        functools.partial(_ffn_kernel, pre=pre, post=post),
        grid=(m // tm, ff // tf),
        in_specs=[
            pl.BlockSpec((tm, d), lambda i, f: (i, 0)),
            pl.BlockSpec(g.shape, lambda i, f: (0, 0)),
            pl.BlockSpec((d, tf), lambda i, f: (0, f)),
            pl.BlockSpec((d, tf), lambda i, f: (0, f)),
            pl.BlockSpec((tf, d), lambda i, f: (f, 0)),
        ],
        out_specs=pl.BlockSpec((tm, d), lambda i, f: (i, 0)),
        out_shape=jax.ShapeDtypeStruct((m, d), F32),
        scratch_shapes=[pltpu.VMEM((tm, d), BF16), pltpu.VMEM((tm, d), F32)],
        compiler_params=_cparams(("parallel", "arbitrary")),
        name="ffn",
    )(x, g, wg, wu, wd)


AB_Q_SB, AB_KV_SB, AB_Q_FX, AB_KV_FX, AB_F = 0, 512, 1536, 2048, 3072
AB_WIDTH_PAD = 3200


def _proj_ab_kernel(x_ref, g_ref, w_ref, b_ref, qsb_ref, kvsb_ref, qfx_ref, kvfx_ref, lf_ref):
    h = _rms(x_ref[...], g_ref[NG_MIX_PRE:NG_MIX_PRE + 1, :]).astype(BF16)
    qsb_ref[...] = _dot(h, w_ref[:, AB_Q_SB:AB_KV_SB])
    kvsb_ref[...] = _dot(h, w_ref[:, AB_KV_SB:AB_Q_FX])
    qfx_ref[...] = _dot(h, w_ref[:, AB_Q_FX:AB_KV_FX])
    kvfx_ref[...] = _dot(h, w_ref[:, AB_KV_FX:AB_F])
    lf_ref[...] = _log_sigmoid(_dot(h, w_ref[:, AB_F:AB_WIDTH_PAD]) + b_ref[...])


def _proj_ab(x, g, w_pad, b_pad):
    m, d = x.shape
    tm = _pick_tile(m, 512)
    widths = (512, 1024, 512, 1024, LANES)
    return pl.pallas_call(
        _proj_ab_kernel,
        grid=(m // tm,),
        in_specs=[
            pl.BlockSpec((tm, d), lambda i: (i, 0)),
            pl.BlockSpec(g.shape, lambda i: (0, 0)),
            pl.BlockSpec(w_pad.shape, lambda i: (0, 0)),
            pl.BlockSpec(b_pad.shape, lambda i: (0, 0)),
        ],
        out_specs=[pl.BlockSpec((tm, w), lambda i: (i, 0)) for w in widths],
        out_shape=[jax.ShapeDtypeStruct((m, w), F32) for w in widths],
        compiler_params=_cparams(("parallel",)),
        name="proj_ab",
    )(x, g, w_pad, b_pad)


CD_QKV_G, CD_R_G, CD_Q_D, CD_KV_D, CD_IQ, CD_IK2, CD_MISC, CD_WIDTH_PAD = 0, 1024, 1536, 2048, 3072, 3584, 3712, 3840
MISC_IW, MISC_GLR = 0, 8


def _proj_cd_kernel(x_ref, g_ref, w_ref, wgu_ref, bg_ref, qkvg_ref, rg_ref, qd_ref, kvd_ref, iq_ref, ik2_ref,
                    misc_ref, la_ref):
    h = _rms(x_ref[...], g_ref[NG_MIX_PRE:NG_MIX_PRE + 1, :]).astype(BF16)
    qkvg_ref[...] = _dot(h, w_ref[:, CD_QKV_G:CD_R_G])
    rg_ref[...] = _dot(h, w_ref[:, CD_R_G:CD_Q_D])
    qd_ref[...] = _dot(h, w_ref[:, CD_Q_D:CD_KV_D])
    kvd_ref[...] = _dot(h, w_ref[:, CD_KV_D:CD_IQ])
    iq_ref[...] = _dot(h, w_ref[:, CD_IQ:CD_IK2])
    ik2_ref[...] = _dot(h, w_ref[:, CD_IK2:CD_MISC])
    misc = _dot(h, w_ref[:, CD_MISC:CD_WIDTH_PAD])
    misc_ref[...] = misc
    la_ref[...] = _log_sigmoid(_dot(misc.astype(BF16), wgu_ref[...]) + bg_ref[...]) / GLA_TAU


def _proj_cd(x, g, w_pad, wgu_pad, bg):
    m, d = x.shape
    tm = _pick_tile(m, 512)
    widths = (1024, 512, 512, 1024, 512, LANES, LANES, H_GLA * DK_GLA)
    return pl.pallas_call(
        _proj_cd_kernel,
        grid=(m // tm,),
        in_specs=[
            pl.BlockSpec((tm, d), lambda i: (i, 0)),
            pl.BlockSpec(g.shape, lambda i: (0, 0)),
            pl.BlockSpec(w_pad.shape, lambda i: (0, 0)),
            pl.BlockSpec(wgu_pad.shape, lambda i: (0, 0)),
            pl.BlockSpec(bg.shape, lambda i: (0, 0)),
        ],
        out_specs=[pl.BlockSpec((tm, w), lambda i: (i, 0)) for w in widths],
        out_shape=[jax.ShapeDtypeStruct((m, w), F32) for w in widths],
        compiler_params=_cparams(("parallel",)),
        name="proj_cd",
    )(x, g, w_pad, wgu_pad, bg)


def _mixout_kernel(x_ref, a_ref, b_ref, g_ref, wo_ref, o_ref):
    wa = a_ref.shape[1]
    y = _dot(a_ref[...].astype(BF16), wo_ref[:wa, :]) + _dot(b_ref[...].astype(BF16), wo_ref[wa:, :])
    o_ref[...] = x_ref[...] + _rms(y, g_ref[NG_MIX_POST:NG_MIX_POST + 1, :])


def _mixout(x, mix_a, mix_b, g, wo):
    m, d = x.shape
    tm = _pick_tile(m, 512)
    return pl.pallas_call(
        _mixout_kernel,
        grid=(m // tm,),
        in_specs=[
            pl.BlockSpec((tm, d), lambda i: (i, 0)),
            pl.BlockSpec((tm, mix_a.shape[1]), lambda i: (i, 0)),
            pl.BlockSpec((tm, mix_b.shape[1]), lambda i: (i, 0)),
            pl.BlockSpec(g.shape, lambda i: (0, 0)),
            pl.BlockSpec(wo.shape, lambda i: (0, 0)),
        ],
        out_specs=pl.BlockSpec((tm, d), lambda i: (i, 0)),
        out_shape=jax.ShapeDtypeStruct((m, d), F32),
        compiler_params=_cparams(("parallel",)),
        name="mixout",
    )(x, mix_a, mix_b, g, wo)


def _ple_kernel(x_ref, p_ref, g_ref, wpp_ref, wpg_ref, o_ref):
    x = x_ref[...]
    gate = jax.nn.sigmoid(_dot(_rms(x, g_ref[NG_PLE_PRE:NG_PLE_PRE + 1, :]).astype(BF16), wpg_ref[...]))
    proj = _dot(p_ref[...].astype(BF16), wpp_ref[...])
    o_ref[...] = x + _rms(proj * gate, g_ref[NG_PLE_POST:NG_PLE_POST + 1, :])


def _ple(x, p, g, wpp, wpg):
    m, d = x.shape
    tm = _pick_tile(m, 512)
    return pl.pallas_call(
        _ple_kernel,
        grid=(m // tm,),
        in_specs=[
            pl.BlockSpec((tm, d), lambda i: (i, 0)),
            pl.BlockSpec((tm, p.shape[1]), lambda i: (i, 0)),
            pl.BlockSpec(g.shape, lambda i: (0, 0)),
            pl.BlockSpec(wpp.shape, lambda i: (0, 0)),
            pl.BlockSpec(wpg.shape, lambda i: (0, 0)),
        ],
        out_specs=pl.BlockSpec((tm, d), lambda i: (i, 0)),
        out_shape=jax.ShapeDtypeStruct((m, d), F32),
        compiler_params=_cparams(("parallel",)),
        name="ple",
    )(x, p, g, wpp, wpg)


def _pair_masks(q):
    lane = _iota(q.shape, 1)
    return (jnp.where(lane < HEAD_DIM, q, 0.0).astype(BF16), jnp.where(lane >= HEAD_DIM, q, 0.0).astype(BF16))


def _sb_prompt_kernel(q_ref, k_ref, vt_ref, o_ref, acc_scr, c_scr, *, tq):
    i = pl.program_id(1)
    q = q_ref[...] * HEAD_DIM ** -0.5
    qm = []
    for p in range(H_SB // 2):
        qm.extend(_pair_masks(q[:, p * LANES:(p + 1) * LANES]))
    row = _iota((tq, tq), 0)
    col = _iota((tq, tq), 1)
    later_neg = jnp.where(col > row, -1.0, 0.0).astype(BF16)
    acc_scr[...] = jnp.zeros_like(acc_scr)
    c_scr[...] = jnp.zeros_like(c_scr)

    def block(kb, diag):
        ks = pl.multiple_of(kb * tq, tq)
        z2 = []
        for p in range(H_SB // 2):
            kblk = k_ref[pl.ds(ks, tq), p * LANES:(p + 1) * LANES]
            z2.extend(_dot_nt(kblk, qm[2 * p + hh]) * LOG2_E for hh in range(2))
        l2 = []
        for zt in z2:
            lt = jnp.maximum(zt, 0.0) + jnp.log2(1.0 + jnp.exp2(-jnp.abs(zt)))
            l2.append(jnp.where(row < col, lt, 0.0) if diag else lt)
        aft = [_dot(later_neg, lt.astype(BF16)) for lt in l2]
        w = []
        for h in range(H_SB):
            c = c_scr[h]
            wt = jnp.exp2(z2[h] - l2[h] + aft[h] + c)
            w.append((jnp.where(row < col, wt, 0.0) if diag else wt).astype(BF16))
            c_scr[h] = c + aft[h][0:1, :] - l2[h][0:1, :]
        for h in range(H_SB):
            acc_scr[h] += _dot(vt_ref[h * HEAD_DIM:(h + 1) * HEAD_DIM, pl.ds(ks, tq)], w[h])

    block(i, True)

    def body(j, carry):
        block(i - 1 - j, False)
        return carry

    lax.fori_loop(0, i, body, 0)
    o_ref[...] = acc_scr[...].reshape(H_SB * HEAD_DIM, tq).T


def _sb_prompt(q, k_bf16, vt_bf16, nb, t):
    tq = _pick_tile(t, 256)
    nq = t // tq
    single = pl.Buffered(1)
    return pl.pallas_call(
        functools.partial(_sb_prompt_kernel, tq=tq),
        grid=(nb, nq),
        in_specs=[
            pl.BlockSpec((tq, q.shape[1]), lambda b, i: (b * nq + i, 0)),
            pl.BlockSpec((t, k_bf16.shape[1]), lambda b, i: (b, 0), pipeline_mode=single),
            pl.BlockSpec((vt_bf16.shape[0], t), lambda b, i: (0, b), pipeline_mode=single),
        ],
        out_specs=pl.BlockSpec((tq, q.shape[1]), lambda b, i: (b * nq + i, 0)),
        out_shape=jax.ShapeDtypeStruct(q.shape, F32),
        scratch_shapes=[pltpu.VMEM((H_SB, HEAD_DIM, tq), F32), pltpu.VMEM((H_SB, 1, tq), F32)],
        compiler_params=_cparams(("parallel", "arbitrary")),
        name="sb_prompt",
    )(q, k_bf16, vt_bf16)


def _cumsum_kernel(x_ref, o_ref, carry_scr, *, tc):
    @pl.when(pl.program_id(1) == 0)
    def _():
        carry_scr[...] = jnp.zeros_like(carry_scr)

    row = _iota((tc, tc), 0)
    col = _iota((tc, tc), 1)
    lower = jnp.where(row >= col, 1.0, 0.0).astype(BF16)
    hi, mid, lo = _split3(x_ref[...])
    cs = _dot(lower, hi) + _dot(lower, mid) + _dot(lower, lo) + carry_scr[...]
    carry_scr[...] = cs[tc - 1:tc, :]
    c_hi = cs.astype(BF16).astype(F32)
    c_mid = (cs - c_hi).astype(BF16).astype(F32)
    c_lo = cs - c_hi - c_mid
    lane = _iota(cs.shape, 1)
    o_ref[...] = jnp.where(lane < H_FOX, c_hi, jnp.where(lane < 2 * H_FOX, c_mid,
                                                          jnp.where(lane < 3 * H_FOX, c_lo, 0.0)))


def _cumsum_rows(x, nb, t):
    tc = _pick_tile(t, 512)
    nt = t // tc
    return pl.pallas_call(
        functools.partial(_cumsum_kernel, tc=tc),
        grid=(nb, nt),
        in_specs=[pl.BlockSpec((tc, LANES), lambda b, j: (b * nt + j, 0))],
        out_specs=pl.BlockSpec((tc, LANES), lambda b, j: (b * nt + j, 0)),
        out_shape=jax.ShapeDtypeStruct((nb * t, LANES), F32),
        scratch_shapes=[pltpu.VMEM((1, LANES), F32)],
        compiler_params=_cparams(("parallel", "arbitrary")),
        name="cumsum_rows",
    )(x)


def _softmax_tiles(logits, values, slots, acc_scr, m_scr, l_scr):
    staged = []
    for s, h in zip(logits, slots):
        m_old = m_scr[h]
        m_new = jnp.maximum(m_old, jnp.max(s, axis=0, keepdims=True))
        alpha = jnp.exp(m_old - m_new)
        pr = jnp.exp(s - m_new)
        l_scr[h] = alpha * l_scr[h] + jnp.sum(pr, axis=0, keepdims=True)
        m_scr[h] = m_new
        staged.append((alpha, pr.astype(BF16)))
    for (alpha, pr), vth, h in zip(staged, values, slots):
        acc_scr[h] = alpha * acc_scr[h] + _dot(vth, pr)


def _fox_prompt_kernel(q_ref, k_ref, vt_ref, cs_ref, o_ref, acc_scr, m_scr, l_scr, *, tq):
    i = pl.program_id(1)
    qs = q_ref[...] * HEAD_DIM ** -0.5
    lane = _iota((tq, LANES), 1)
    qaug = []
    for head in range(H_FOX):
        pick = jnp.where((lane == head) | (lane == head + H_FOX) | (lane == head + 2 * H_FOX), -1.0, 0.0)
        own = (lane < HEAD_DIM) if head % 2 == 0 else (lane >= HEAD_DIM)
        q_pair = qs[:, (head // 2) * LANES:(head // 2 + 1) * LANES]
        qaug.append(jnp.concatenate([jnp.where(own, q_pair, 0.0), pick], axis=1).astype(BF16))
    row = _iota((tq, tq), 0)
    col = _iota((tq, tq), 1)
    acc_scr[...] = jnp.zeros_like(acc_scr)
    m_scr[...] = jnp.full_like(m_scr, NEG_BIG)
    l_scr[...] = jnp.zeros_like(l_scr)

    def block(kb, diag):
        ks = pl.multiple_of(kb * tq, tq)
        cs = cs_ref[pl.ds(ks, tq), :]
        logits = []
        for p in range(H_FOX // 2):
            kaug = jnp.concatenate([k_ref[pl.ds(ks, tq), p * LANES:(p + 1) * LANES], cs], axis=1)
            logits.extend(_dot_nt(kaug, qaug[2 * p + hh]) for hh in range(2))
        if diag:
            logits = [jnp.where(row <= col, s, NEG_BIG) for s in logits]
        values = [vt_ref[h * HEAD_DIM:(h + 1) * HEAD_DIM, pl.ds(ks, tq)] for h in range(H_FOX)]
        _softmax_tiles(logits, values, tuple(range(H_FOX)), acc_scr, m_scr, l_scr)

    block(i, True)

    def body(j, carry):
        block(j, False)
        return carry

    lax.fori_loop(0, i, body, 0)
    o_ref[...] = (acc_scr[...] / l_scr[...]).reshape(H_FOX * HEAD_DIM, tq).T


def _fox_prompt(q, k_bf16, vt_bf16, cum_split_bf16, nb, t):
    tq = _pick_tile(t, 256)
    nq = t // tq
    single = pl.Buffered(1)
    return pl.pallas_call(
        functools.partial(_fox_prompt_kernel, tq=tq),
        grid=(nb, nq),
        in_specs=[
            pl.BlockSpec((tq, q.shape[1]), lambda b, i: (b * nq + i, 0)),
            pl.BlockSpec((t, k_bf16.shape[1]), lambda b, i: (b, 0), pipeline_mode=single),
            pl.BlockSpec((vt_bf16.shape[0], t), lambda b, i: (0, b), pipeline_mode=single),
            pl.BlockSpec((t, LANES), lambda b, i: (b, 0), pipeline_mode=single),
        ],
        out_specs=pl.BlockSpec((tq, q.shape[1]), lambda b, i: (b * nq + i, 0)),
        out_shape=jax.ShapeDtypeStruct(q.shape, F32),
        scratch_shapes=[pltpu.VMEM((H_FOX, HEAD_DIM, tq), F32), pltpu.VMEM((H_FOX, 1, tq), F32),
                        pltpu.VMEM((H_FOX, 1, tq), F32)],
        compiler_params=_cparams(("parallel", "arbitrary")),
        name="fox_prompt",
    )(q, k_bf16, vt_bf16, cum_split_bf16)


def _gla_kernel(qkv_ref, la_ref, r_ref, s0_ref, gn_ref, o_ref, sfin_ref, st_scr, o_scr, e_scr, *, chunk, tt):
    j = pl.program_id(1)
    nk = H_GLA * DK_GLA
    nv = H_GLA * DV_GLA
    c = chunk

    @pl.when(j == 0)
    def _():
        st_scr[...] = s0_ref[0]

    expand = jnp.where(_iota((nk, nv), 0) // DK_GLA == _iota((nk, nv), 1) // DV_GLA, 1.0, 0.0).astype(BF16)
    diag_t = jnp.where(_iota((nv, nk), 0) // DV_GLA == _iota((nv, nk), 1) // DK_GLA, 1.0, 0.0)
    rowc = _iota((c, nk), 0)

    def chunk_body(ci, carry):
        r0 = pl.multiple_of(ci * c, c)
        q = qkv_ref[0, pl.ds(r0, c), 0:nk] * DK_GLA ** -0.5
        k = qkv_ref[0, pl.ds(r0, c), nk:2 * nk]
        v = qkv_ref[0, pl.ds(r0, c), 2 * nk:2 * nk + nv]
        g = la_ref[0, pl.ds(r0, c), :]
        b = jnp.zeros((c, nk), F32)
        for s in range(c):
            b = b + jnp.where(rowc >= s, g[s:s + 1, :], 0.0)
        for s in range(c):
            decay = jnp.where(rowc >= s, jnp.exp(b - b[s:s + 1, :]), 0.0)
            e_scr[s * c:(s + 1) * c, :] = decay * q * k[s:s + 1, :]
        hi, lo = _split2(e_scr[...])
        att = _dot(hi, expand) + _dot(lo, expand)
        st = st_scr[...]
        o = _dot_nt((q * jnp.exp(b)).astype(BF16), st.astype(BF16))
        for s in range(c):
            o = o + att[s * c:(s + 1) * c, :] * v[s:s + 1, :]
        o_scr[pl.ds(r0, c), :] = o
        b_last = b[c - 1:c, :]
        kd = k * jnp.exp(b_last - b)
        st_scr[...] = st * jnp.exp(b_last) + diag_t * _dot_tn(v.astype(BF16), kd.astype(BF16))
        return carry

    lax.fori_loop(0, tt // c, chunk_body, 0)

    o = o_scr[...]
    r = r_ref[0]
    gn = gn_ref[...]
    for h in range(H_GLA):
        sl = slice(h * DV_GLA, (h + 1) * DV_GLA)
        oh = o[:, sl]
        rh = r[:, sl]
        y = oh * lax.rsqrt(jnp.mean(oh * oh, axis=-1, keepdims=True) + EPS) * gn
        o_ref[0, :, sl] = y * (rh * jax.nn.sigmoid(rh))

    @pl.when(j == pl.num_programs(1) - 1)
    def _():
        sfin_ref[0] = st_scr[...]


def _gla(qkv, la, r, s0_t, gnorm):
    nb, t, _ = qkv.shape
    chunk = min(GLA_CHUNK, t)
    tt = _pick_tile(t, 512)
    nk, nv = H_GLA * DK_GLA, H_GLA * DV_GLA
    return pl.pallas_call(
        functools.partial(_gla_kernel, chunk=chunk, tt=tt),
        grid=(nb, t // tt),
        in_specs=[
            pl.BlockSpec((1, tt, qkv.shape[2]), lambda b, j: (b, j, 0)),
            pl.BlockSpec((1, tt, nk), lambda b, j: (b, j, 0)),
            pl.BlockSpec((1, tt, nv), lambda b, j: (b, j, 0)),
            pl.BlockSpec((1, nv, nk), lambda b, j: (b, 0, 0)),
            pl.BlockSpec(gnorm.shape, lambda b, j: (0, 0)),
        ],
        out_specs=[
            pl.BlockSpec((1, tt, nv), lambda b, j: (b, j, 0)),
            pl.BlockSpec((1, nv, nk), lambda b, j: (b, 0, 0)),
        ],
        out_shape=[jax.ShapeDtypeStruct((nb, t, nv), F32), jax.ShapeDtypeStruct(s0_t.shape, F32)],
        scratch_shapes=[pltpu.VMEM((nv, nk), F32), pltpu.VMEM((tt, nv), F32), pltpu.VMEM((chunk * chunk, nk), F32)],
        compiler_params=_cparams(("parallel", "arbitrary")),
        name="gla",
    )(qkv, la, r, s0_t, gnorm)


def _t5_bucket(dist):
    max_exact = N_BUCKETS // 2
    d = jnp.maximum(dist, 0)
    large = max_exact + (jnp.log(jnp.maximum(d, 1).astype(F32) / max_exact)
                         / math.log(MAX_DISTANCE / max_exact) * (N_BUCKETS - max_exact)).astype(I32)
    return jnp.where(d < max_exact, d, jnp.minimum(large, N_BUCKETS - 1))


def _bias_tables_kernel(rb_ref, bp_ref, bs_ref, *, tq, past):
    bucket_p = _t5_bucket(_iota((2 * tq, tq), 1) - _iota((2 * tq, tq), 0) + tq)
    bucket_s = _t5_bucket(past + _iota(bs_ref.shape, 0) // H_DSA - _iota(bs_ref.shape, 1))
    head_s = _iota(bs_ref.shape, 0) % H_DSA
    acc_s = jnp.zeros(bs_ref.shape, F32)
    for h in range(H_DSA):
        acc_p = jnp.zeros((2 * tq, tq), F32)
        for bkt in range(N_BUCKETS):
            val = rb_ref[bkt, h]
            acc_p = jnp.where(bucket_p == bkt, val, acc_p)
            acc_s = jnp.where((bucket_s == bkt) & (head_s == h), val, acc_s)
        bp_ref[h] = acc_p
    bs_ref[...] = acc_s


def _bias_tables(rel_bias, tq, past, n_new):
    cols_s = past + LANES
    return pl.pallas_call(
        functools.partial(_bias_tables_kernel, tq=tq, past=past),
        in_specs=[pl.BlockSpec(memory_space=pltpu.SMEM)],
        out_specs=[pl.BlockSpec(memory_space=pltpu.VMEM), pl.BlockSpec(memory_space=pltpu.VMEM)],
        out_shape=[jax.ShapeDtypeStruct((H_DSA, 2 * tq, tq), F32),
                   jax.ShapeDtypeStruct((n_new * H_DSA, cols_s), F32)],
        compiler_params=pltpu.CompilerParams(vmem_limit_bytes=VMEM_LIMIT_BYTES),
        name="bias_tables",
    )(rel_bias)


def _sort_key(score):
    bits = lax.bitcast_convert_type(score + 0.0, I32)
    return bits ^ ((bits >> 31) & 0x7FFFFFFF)


I16 = jnp.int16
HALF_MIN = -2 ** 15


def _kth_largest_by_halves(key_scr, hi_scr, lo_scr, n_chunks, chunk, kf, n_queries):
    one, zero = jnp.ones((), BF16), jnp.zeros((), BF16)
    groups = chunk // 16

    def count_ge(scr, cand):
        cand16 = cand.astype(I16)

        def body(cb, acc):
            kk = scr[pl.ds(pl.multiple_of(cb * chunk, chunk), chunk), :]
            hit = jnp.where(kk >= cand16, one, zero)
            parts = [hit[g * 16:(g + 1) * 16, :] for g in range(groups)]
            while len(parts) > 1:
                parts = [a + b for a, b in zip(parts[0::2], parts[1::2])]
            return acc + parts[0].astype(F32)

        acc = lax.fori_loop(0, n_chunks, body, jnp.zeros((16, n_queries), F32))
        return jnp.sum(acc, axis=0, keepdims=True)

    def search(scr, want):
        thr0 = jnp.where(count_ge(scr, jnp.zeros((1, n_queries), I32)) >= want, 0, HALF_MIN).astype(I32)

        def bit_body(t, thr):
            cand = thr + (jnp.int32(1) << (14 - t))
            return jnp.where(count_ge(scr, cand) >= want, cand, thr)

        return lax.fori_loop(0, 15, bit_body, thr0)

    thr_hi = search(hi_scr, kf)
    above = count_ge(hi_scr, jnp.minimum(thr_hi + 1, -HALF_MIN - 1))
    above = jnp.where(thr_hi == -HALF_MIN - 1, 0.0, above)

    def fill_lo(cb, carry):
        sl = pl.ds(pl.multiple_of(cb * chunk, chunk), chunk)
        k32 = key_scr[sl, :]
        lo = (k32 & 0xFFFF) + HALF_MIN
        lo_scr[sl, :] = jnp.where((k32 >> 16) == thr_hi, lo, HALF_MIN).astype(I16)
        return carry

    lax.fori_loop(0, n_chunks, fill_lo, 0)
    thr_lo = search(lo_scr, kf - above)
    return thr_hi * 65536 + (thr_lo - HALF_MIN)


def _select_threshold(key_scr, n_chunks, chunk, topk, n_queries, key_axis, halves=None):
    if key_axis == 1:
        chunk_shape, vec_shape = (n_queries, chunk), (n_queries, 1)

        def chunk_slice(cb):
            return (slice(None), pl.ds(pl.multiple_of(cb * chunk, chunk), chunk))

        def count(pred):
            def body(cb, acc):
                idx = cb * chunk + _iota(chunk_shape, 1)
                return acc + jnp.where(pred(key_scr[chunk_slice(cb)], idx), 1.0, 0.0)

            acc = lax.fori_loop(0, n_chunks, body, jnp.zeros(chunk_shape, F32))
            return jnp.sum(acc, axis=1, keepdims=True)
    else:
        chunk_shape, vec_shape = (chunk, n_queries), (1, n_queries)

        def chunk_slice(cb):
            return (pl.ds(pl.multiple_of(cb * chunk, chunk), chunk), slice(None))

        def count(pred):
            def body(cb, acc):
                idx = cb * chunk + _iota(chunk_shape, 0)
                hit = jnp.where(pred(key_scr[chunk_slice(cb)], idx), 1.0, 0.0)
                return acc + jnp.sum(hit.reshape(chunk // 8, 8, n_queries), axis=0)

            acc = lax.fori_loop(0, n_chunks, body, jnp.zeros((8, n_queries), F32))
            return jnp.sum(acc, axis=0, keepdims=True)

    kf = float(topk)
    if halves is None:
        thr0 = jnp.where(count(lambda kk, idx: kk >= 0) >= kf, 0, INT_MIN).astype(I32)

        def bit_body(t, thr):
            cand = thr + (jnp.int32(1) << (30 - t))
            return jnp.where(count(lambda kk, idx: kk >= cand) >= kf, cand, thr)

        thr = lax.fori_loop(0, 31, bit_body, thr0)
    else:
        thr = _kth_largest_by_halves(key_scr, halves[0], halves[1], n_chunks, chunk, kf, n_queries)
    n_gt = count(lambda kk, idx: kk > thr)
    n_eq = count(lambda kk, idx: kk == thr)
    need = kf - n_gt
    trim = (n_eq > need) & (thr > INT_MIN)

    @pl.when(jnp.max(jnp.where(trim, 1.0, 0.0)) > 0.0)
    def _():
        n_bits = max(1, int(math.ceil(math.log2(key_scr.shape[key_axis] + 1))))

        def idx_body(t, cut):
            cand = cut + (jnp.int32(1) << (n_bits - 1 - t))
            return jnp.where(count(lambda kk, idx: (kk == thr) & (idx < cand)) < need, cand, cut)

        cut = lax.fori_loop(0, n_bits, idx_body, jnp.zeros(vec_shape, I32))

        def demote(cb, carry):
            kk = key_scr[chunk_slice(cb)]
            idx = cb * chunk + _iota(chunk_shape, key_axis)
            key_scr[chunk_slice(cb)] = jnp.where(trim & (kk == thr) & (idx > cut), kk - 1, kk)
            return carry

        lax.fori_loop(0, n_chunks, demote, 0)

    return jnp.maximum(thr, INT_MIN + 1)


def _dsa_prompt_kernel(rb_ref, q_ref, iq_ref, iwt_ref, k_ref, vt_ref, ik_ref, bias_ref, o_ref,
                       key_scr, khi_scr, klo_scr, acc_scr, m_scr, l_scr, *, tq, topk):
    i = pl.program_id(1)
    npair = H_DSA // 2
    row = _iota((tq, tq), 0)
    col = _iota((tq, tq), 1)

    iq = iq_ref[...]
    iqm = []
    for p in range(H_IDX // 2):
        iqm.extend(_pair_masks(iq[:, p * LANES:(p + 1) * LANES]))
    wq = [iwt_ref[MISC_IW + h:MISC_IW + h + 1, :] * H_IDX ** -0.5 for h in range(H_IDX)]

    def score_block(kb, diag):
        ks = pl.multiple_of(kb * tq, tq)
        ik2 = ik_ref[pl.ds(ks, tq), :]
        score = jnp.zeros((tq, tq), F32)
        for h in range(H_IDX):
            score = score + jnp.maximum(_dot_nt(ik2, iqm[h]), 0.0) * wq[h]
        key = _sort_key(score)
        if diag:
            key = jnp.where(row <= col, key, INT_MIN)
        key_scr[pl.ds(ks, tq), :] = key
        khi_scr[pl.ds(ks, tq), :] = (key >> 16).astype(I16)

    score_block(i, True)

    def score_body(kb, carry):
        score_block(kb, False)
        return carry

    lax.fori_loop(0, i, score_body, 0)

    thr = _select_threshold(key_scr, i + 1, tq, topk, tq, key_axis=0, halves=(khi_scr, klo_scr))

    q = q_ref[...] * HEAD_DIM ** -0.5
    qm = []
    for p in range(npair):
        qm.extend(_pair_masks(q[:, p * LANES:(p + 1) * LANES]))
    acc_scr[...] = jnp.zeros_like(acc_scr)
    m_scr[...] = jnp.full_like(m_scr, NEG_BIG)
    l_scr[...] = jnp.zeros_like(l_scr)

    def attend_block(kb, near):
        ks = pl.multiple_of(kb * tq, tq)
        sel = key_scr[pl.ds(ks, tq), :] >= thr
        logits, values = [], []
        for p in range(npair):
            kblk = k_ref[pl.ds(ks, tq), p * LANES:(p + 1) * LANES]
            for hh in range(2):
                h = 2 * p + hh
                if near == 0:
                    bias = rb_ref[N_BUCKETS - 1, h]
                else:
                    bias = bias_ref[h, (near - 1) * tq:near * tq, :]
                logits.append(jnp.where(sel, _dot_nt(kblk, qm[h]) + bias, NEG_BIG))
                values.append(vt_ref[h * HEAD_DIM:(h + 1) * HEAD_DIM, pl.ds(ks, tq)])
        _softmax_tiles(logits, values, tuple(range(H_DSA)), acc_scr, m_scr, l_scr)

    attend_block(i, 2)

    @pl.when(i >= 1)
    def _():
        attend_block(i - 1, 1)

    def attend_body(kb, carry):
        attend_block(kb, 0)
        return carry

    lax.fori_loop(0, jnp.maximum(i - 1, 0), attend_body, 0)
    o_ref[...] = (acc_scr[...] / l_scr[...]).reshape(H_DSA * HEAD_DIM, tq).T


def _dsa_prompt(rel_bias, q, iq, iw_t, k_bf16, vt_bf16, ik2_bf16, bias_p, nb, t, tq):
    nq = t // tq
    topk = min(DSA_TOPK, t // 4)
    single = pl.Buffered(1)
    return pl.pallas_call(
        functools.partial(_dsa_prompt_kernel, tq=tq, topk=topk),
        grid=(nb, nq),
        in_specs=[
            pl.BlockSpec(memory_space=pltpu.SMEM),
            pl.BlockSpec((tq, q.shape[1]), lambda b, i: (b * nq + i, 0)),
            pl.BlockSpec((tq, iq.shape[1]), lambda b, i: (b * nq + i, 0)),
            pl.BlockSpec((LANES, tq), lambda b, i: (0, b * nq + i)),
            pl.BlockSpec((t, k_bf16.shape[1]), lambda b, i: (b, 0), pipeline_mode=single),
            pl.BlockSpec((vt_bf16.shape[0], t), lambda b, i: (0, b), pipeline_mode=single),
            pl.BlockSpec((t, LANES), lambda b, i: (b, 0), pipeline_mode=single),
            pl.BlockSpec(bias_p.shape, lambda b, i: (0, 0, 0), pipeline_mode=single),
        ],
        out_specs=pl.BlockSpec((tq, q.shape[1]), lambda b, i: (b * nq + i, 0)),
        out_shape=jax.ShapeDtypeStruct(q.shape, F32),
        scratch_shapes=[pltpu.VMEM((t, tq), I32), pltpu.VMEM((t, tq), I16), pltpu.VMEM((t, tq), I16),
                        pltpu.VMEM((H_DSA, HEAD_DIM, tq), F32),
                        pltpu.VMEM((H_DSA, 1, tq), F32), pltpu.VMEM((H_DSA, 1, tq), F32)],
        compiler_params=_cparams(("parallel", "arbitrary")),
        name="dsa_prompt",
    )(rel_bias, q, iq, iw_t, k_bf16, vt_bf16, ik2_bf16, bias_p)


def _rows_by_head(x, n_heads):
    t, w = x.shape
    head_mask = _iota((n_heads, w), 1) // HEAD_DIM == _iota((n_heads, w), 0)
    return jnp.concatenate([jnp.where(head_mask, x[i:i + 1, :], 0.0) for i in range(t)], axis=0)


def _collapse_heads(acc, n_new, n_heads):
    w = acc.shape[1]
    head_mask = _iota((n_heads, w), 1) // HEAD_DIM == _iota((n_heads, w), 0)
    rows = [jnp.sum(jnp.where(head_mask, acc[i * n_heads:(i + 1) * n_heads, :], 0.0), axis=0, keepdims=True)
            for i in range(n_new)]
    return jnp.concatenate(rows, axis=0)


def _sb_sample_kernel(pt_ref, q_ref, kvn_ref, *refs, n_pages, n_new):
    del pt_ref
    kt_pages = refs[:n_pages]
    vt_pages = refs[n_pages:2 * n_pages]
    o_ref = refs[2 * n_pages]
    hw = H_SB * HEAD_DIM
    qbd = _rows_by_head(q_ref[0] * HEAD_DIM ** -0.5, H_SB)
    qbd_b = qbd.astype(BF16)
    nr = n_new * H_SB
    trow = _iota((nr, 1), 0) // H_SB
    kn = kvn_ref[0, :, :hw]
    vn = kvn_ref[0, :, hw:]
    lg_new, z_new = [], []
    for j in range(n_new):
        z = jnp.sum(qbd * kn[j:j + 1, :], axis=1, keepdims=True)
        z_new.append(z)
        lg_new.append(jnp.where(j < trow, _log_sigmoid(-z), 0.0))
    acc = jnp.zeros((nr, hw), F32)
    carry = jnp.zeros((nr, 1), F32)
    for j in reversed(range(n_new)):
        w = jnp.where(j < trow, jnp.exp(z_new[j] + lg_new[j] + carry), 0.0)
        acc = acc + w * vn[j:j + 1, :]
        carry = carry + lg_new[j]
    upper = jnp.where(_iota((LANES, LANES), 0) > _iota((LANES, LANES), 1), 1.0, 0.0).astype(BF16)
    z = [_dot(qbd_b, kt_pages[pg][0, 0, 0].astype(BF16)) for pg in range(n_pages)]
    lg = [_log_sigmoid(-zp) for zp in z]
    aft = []
    for lgp in lg:
        hi, lo = _split2(lgp)
        aft.append(_dot(hi, upper) + _dot(lo, upper))
    for pg in reversed(range(n_pages)):
        w = jnp.exp(z[pg] + lg[pg] + aft[pg] + carry)
        acc = acc + _dot_nt(w.astype(BF16), vt_pages[pg][0, 0, 0].astype(BF16))
        carry = carry + aft[pg][:, 0:1] + lg[pg][:, 0:1]
    o_ref[0] = _collapse_heads(acc, n_new, H_SB)


def _page_specs(n_pages, layer, block, tail):
    def spec(j):
        return pl.BlockSpec(block, lambda b, pt: (layer, pt[b, j]) + tail)
    return [spec(j) for j in range(n_pages)]


def _kv_page_specs(n_pages, layer, cache_t):
    block = (1, 1, 1) + cache_t.shape[3:]
    return _page_specs(n_pages, layer, block, (0, 0, 0)) + _page_specs(n_pages, layer, block, (1, 0, 0))


def _seq_spec(n_new, width):
    return pl.BlockSpec((1, n_new, width), lambda b, pt: (b, 0, 0))


def _sb_sample(page_table, q, kv_new, cache_t, layer):
    db, n_new, hw = q.shape
    n_pages = page_table.shape[1]
    grid_spec = pltpu.PrefetchScalarGridSpec(
        num_scalar_prefetch=1,
        grid=(db,),
        in_specs=[_seq_spec(n_new, hw), _seq_spec(n_new, 2 * hw)] + _kv_page_specs(n_pages, layer, cache_t),
        out_specs=_seq_spec(n_new, hw),
    )
    return pl.pallas_call(
        functools.partial(_sb_sample_kernel, n_pages=n_pages, n_new=n_new),
        grid_spec=grid_spec,
        out_shape=jax.ShapeDtypeStruct(q.shape, F32),
        compiler_params=_cparams(("parallel",)),
        name="sb_sample",
    )(page_table, q, kv_new, *([cache_t] * (2 * n_pages)))


def _head_column(x, n_new, n_heads, pick):
    lane_is_head = _iota((n_heads, x.shape[1]), 1) == _iota((n_heads, x.shape[1]), 0)
    cols = [jnp.sum(jnp.where(lane_is_head, x[pick(i):pick(i) + 1, :], 0.0), axis=1, keepdims=True)
            for i in range(n_new)]
    return jnp.concatenate(cols, axis=0)


def _fox_sample_kernel(pt_ref, q_ref, kvn_ref, lfn_ref, *refs, n_pages, n_new):
    del pt_ref
    kt_pages = refs[:n_pages]
    vt_pages = refs[n_pages:2 * n_pages]
    lf_pages = refs[2 * n_pages:3 * n_pages]
    o_ref = refs[3 * n_pages]
    hw = H_FOX * HEAD_DIM
    qbd = _rows_by_head(q_ref[0] * HEAD_DIM ** -0.5, H_FOX)
    qbd_b = qbd.astype(BF16)
    nr = n_new * H_FOX
    trow = _iota((nr, 1), 0) // H_FOX
    kn = kvn_ref[0, :, :hw]
    vn = kvn_ref[0, :, hw:]
    lfn = lfn_ref[0]
    csum = [lfn[0:1, :]]
    for j in range(1, n_new):
        csum.append(csum[-1] + lfn[j:j + 1, :])
    csum = jnp.concatenate(csum, axis=0)
    c_query = _head_column(csum, n_new, H_FOX, lambda t: t)
    s_new = []
    for j in range(n_new):
        c_key = _head_column(csum, n_new, H_FOX, lambda t, j=j: j)
        z = jnp.sum(qbd * kn[j:j + 1, :], axis=1, keepdims=True)
        s_new.append(jnp.where(j <= trow, z + (c_query - c_key), NEG_BIG))
    upper = jnp.where(_iota((LANES, LANES), 0) > _iota((LANES, LANES), 1), 1.0, 0.0).astype(BF16)
    z = [_dot(qbd_b, kt_pages[pg][0, 0, 0].astype(BF16)) for pg in range(n_pages)]
    lf, aft = [], []
    for pg in range(n_pages):
        lfp = jnp.concatenate([lf_pages[pg][0, 0]] * n_new, axis=0)
        hi, mid, lo = _split3(lfp)
        lf.append(lfp)
        aft.append(_dot(hi, upper) + _dot(mid, upper) + _dot(lo, upper))
    carry = c_query
    s = [None] * n_pages
    for pg in reversed(range(n_pages)):
        s[pg] = z[pg] + aft[pg] + carry
        carry = carry + aft[pg][:, 0:1] + lf[pg][:, 0:1]
    m = s_new[0]
    for j in range(1, n_new):
        m = jnp.maximum(m, s_new[j])
    for pg in range(n_pages):
        m = jnp.maximum(m, jnp.max(s[pg], axis=1, keepdims=True))
    l = jnp.zeros((nr, 1), F32)
    acc = jnp.zeros((nr, hw), F32)
    for j in range(n_new):
        pr = jnp.exp(s_new[j] - m)
        l = l + pr
        acc = acc + pr * vn[j:j + 1, :]
    for pg in range(n_pages):
        pr = jnp.exp(s[pg] - m)
        l = l + jnp.sum(pr, axis=1, keepdims=True)
        acc = acc + _dot_nt(pr.astype(BF16), vt_pages[pg][0, 0, 0].astype(BF16))
    o_ref[0] = _collapse_heads(acc / l, n_new, H_FOX)


def _fox_sample(page_table, q, kv_new, lf_new, cache_t, cache_lf_t, layer):
    db, n_new, hw = q.shape
    n_pages = page_table.shape[1]
    page = cache_lf_t.shape[3]
    grid_spec = pltpu.PrefetchScalarGridSpec(
        num_scalar_prefetch=1,
        grid=(db,),
        in_specs=[_seq_spec(n_new, hw), _seq_spec(n_new, 2 * hw), _seq_spec(n_new, LANES)]
        + _kv_page_specs(n_pages, layer, cache_t)
        + _page_specs(n_pages, layer, (1, 1, H_FOX, page), (0, 0)),
        out_specs=_seq_spec(n_new, hw),
    )
    return pl.pallas_call(
        functools.partial(_fox_sample_kernel, n_pages=n_pages, n_new=n_new),
        grid_spec=grid_spec,
        out_shape=jax.ShapeDtypeStruct(q.shape, F32),
        compiler_params=_cparams(("parallel",)),
        name="fox_sample",
    )(page_table, q, kv_new, lf_new, *([cache_t] * (2 * n_pages)), *([cache_lf_t] * n_pages))


def _dsa_sample_kernel(pt_ref, q_ref, kvn_ref, iq_ref, iw_ref, ikn_ref, bias_ref, *refs, n_pages, n_new, topk):
    del pt_ref
    kt_pages = refs[:n_pages]
    vt_pages = refs[n_pages:2 * n_pages]
    ik_pages = refs[2 * n_pages:3 * n_pages]
    o_ref = refs[3 * n_pages]
    key_scr = refs[3 * n_pages + 1]
    hw = H_DSA * HEAD_DIM
    nr = n_new * H_DSA
    page = ik_pages[0].shape[3]
    past = n_pages * page
    n_rows = key_scr.shape[0]

    iq = iq_ref[0].astype(BF16)
    iw = iw_ref[0] * H_IDX ** -0.5

    def head_sum(x):
        return jnp.sum(x.reshape(n_new, H_IDX, x.shape[1]), axis=1)

    def per_head_rows(x):
        return jnp.concatenate([jnp.broadcast_to(x[i:i + 1, :], (H_DSA, x.shape[1])) for i in range(n_new)], axis=0)

    if n_rows > n_new:
        key_scr[n_new:, :] = jnp.full((n_rows - n_new, key_scr.shape[1]), INT_MIN, I32)
    for pg in range(n_pages):
        ikp = ik_pages[pg][0, 0].astype(BF16)
        sc = head_sum(jnp.maximum(_dot(iq, ikp), 0.0) * iw)
        key_scr[0:n_new, pg * page:(pg + 1) * page] = _sort_key(sc)
    ikn = ikn_ref[0].astype(BF16)
    sc = head_sum(jnp.maximum(_dot(iq, ikn), 0.0) * iw)
    visible = _iota((n_new, LANES), 1) <= _iota((n_new, LANES), 0)
    key_scr[0:n_new, past:past + LANES] = jnp.where(visible, _sort_key(sc), INT_MIN)

    thr = per_head_rows(_select_threshold(key_scr, 1, past + LANES, topk, n_rows, key_axis=1))

    qbd_b = _rows_by_head(q_ref[0] * HEAD_DIM ** -0.5, H_DSA).astype(BF16)
    def keys_t(pg):
        return (kt_pages[pg][0, 0, 0] if pg < n_pages else kvn_ref[0, 0]).astype(BF16)

    def values_t(pg):
        return (vt_pages[pg][0, 0, 0] if pg < n_pages else kvn_ref[0, 1]).astype(BF16)

    s = []
    for pg in range(n_pages + 1):
        sel = per_head_rows(key_scr[:, pg * page:(pg + 1) * page]) >= thr
        s.append(jnp.where(sel, _dot(qbd_b, keys_t(pg)) + bias_ref[:, pg * page:(pg + 1) * page], NEG_BIG))
    m = jnp.max(s[0], axis=1, keepdims=True)
    for pg in range(1, n_pages + 1):
        m = jnp.maximum(m, jnp.max(s[pg], axis=1, keepdims=True))
    l = jnp.zeros((nr, 1), F32)
    acc = jnp.zeros((nr, hw), F32)
    for pg in range(n_pages + 1):
        pr = jnp.exp(s[pg] - m)
        l = l + jnp.sum(pr, axis=1, keepdims=True)
        acc = acc + _dot_nt(pr.astype(BF16), values_t(pg))
    o_ref[0] = _collapse_heads(acc / l, n_new, H_DSA)


def _dsa_sample(page_table, q, kv_new_t, iq_rows, iw_rows, ik_new_t, bias_s, cache_t, cache_ik_t, layer):
    db, n_new, hw = q.shape
    n_pages = page_table.shape[1]
    page = cache_ik_t.shape[3]
    nr = n_new * H_DSA
    key_rows = -(-n_new // 8) * 8
    topk = min(DSA_TOPK, (n_pages * page + n_new) // 4)
    grid_spec = pltpu.PrefetchScalarGridSpec(
        num_scalar_prefetch=1,
        grid=(db,),
        in_specs=[_seq_spec(n_new, hw),
                  pl.BlockSpec((1, 2, hw, page), lambda b, pt: (b, 0, 0, 0)),
                  _seq_spec(nr, D_IDX),
                  _seq_spec(nr, 1),
                  pl.BlockSpec((1, D_IDX, page), lambda b, pt: (b, 0, 0)),
                  pl.BlockSpec(bias_s.shape, lambda b, pt: (0, 0))]
        + _kv_page_specs(n_pages, layer, cache_t)
        + _page_specs(n_pages, layer, (1, 1, D_IDX, page), (0, 0)),
        out_specs=_seq_spec(n_new, hw),
        scratch_shapes=[pltpu.VMEM((key_rows, n_pages * page + LANES), I32)],
    )
    return pl.pallas_call(
        functools.partial(_dsa_sample_kernel, n_pages=n_pages, n_new=n_new, topk=topk),
        grid_spec=grid_spec,
        out_shape=jax.ShapeDtypeStruct(q.shape, F32),
        compiler_params=_cparams(("parallel",)),
        name="dsa_sample",
    )(page_table, q, kv_new_t, iq_rows, iw_rows, ik_new_t, bias_s, *([cache_t] * (2 * n_pages)),
      *([cache_ik_t] * n_pages))


def _state_to_block_diag(s):
    nb = s.shape[0]
    eye = jnp.eye(H_GLA, dtype=s.dtype)
    return jnp.einsum("bhkv,hg->bhvgk", s, eye).reshape(nb, H_GLA * DV_GLA, H_GLA * DK_GLA)


def _block_diag_to_state(bd):
    nb = bd.shape[0]
    bd = bd.reshape(nb, H_GLA, DV_GLA, H_GLA, DK_GLA)
    return jnp.stack([jnp.swapaxes(bd[:, h, :, h, :], 1, 2) for h in range(H_GLA)], axis=1)


def _pad_cols(w, width):
    return jnp.pad(w, ((0, 0), (0, width - w.shape[1])))


def _cd_weight_layout(w):
    gq = H_GLA * DK_GLA
    gv = H_GLA * DV_GLA
    dh = H_DSA * HEAD_DIM
    offs = np.cumsum([0, gq, gq, gv, GLA_RANK, gv, dh, dh, dh, H_IDX * D_IDX, D_IDX, H_IDX])
    seg = [w[:, offs[i]:offs[i + 1]] for i in range(11)]
    q_g, k_g, v_g, g_lr, r_g, q_d, k_d, v_d, iq, ik, iw = seg
    misc = _pad_cols(jnp.concatenate([iw, g_lr], axis=1), LANES)
    return jnp.concatenate([q_g, k_g, v_g, r_g, q_d, k_d, v_d, iq, ik, ik, misc], axis=1)


def kernel(x_prompt, x_sample, p_prompt, p_sample, cache_sb_kv, cache_fox_kv, cache_fox_logf, state_gla,
           cache_dsa_kv, cache_dsa_idxk, page_table, norm_gains, ffn_w_gate, ffn_w_up, ffn_w_down, w_out,
           ple_w_proj, ple_w_gate, ab_w_in, ab_b_forget, cd_w_in, gla_w_gate_up, gla_b_gate, gla_norm, rel_bias):
    nb, t, d = x_prompt.shape
    db, n_new, _ = x_sample.shape
    depth = norm_gains.shape[0]
    bt = nb * t
    n_pages = page_table.shape[1]
    page = cache_sb_kv.shape[2]
    past = n_pages * page
    tq = _pick_tile(t, 256)

    x = jnp.concatenate([x_prompt.reshape(bt, d), x_sample.reshape(db * n_new, d)], axis=0)
    p_all = jnp.concatenate([p_prompt.reshape(depth, bt, -1), p_sample.reshape(depth, db * n_new, -1)], axis=1)

    wg = ffn_w_gate.astype(BF16)
    wu = ffn_w_up.astype(BF16)
    wd = ffn_w_down.astype(BF16)
    wo = w_out.astype(BF16)
    wpp = ple_w_proj.astype(BF16)
    wpg = ple_w_gate.astype(BF16)

    def pages_last(c):
        ct = jnp.transpose(c, (0, 1, 3, 4, 5, 2))
        return ct.reshape(ct.shape[:3] + (-1, ct.shape[5]))

    sb_cache = pages_last(cache_sb_kv)
    fox_cache = pages_last(cache_fox_kv)
    dsa_cache = pages_last(cache_dsa_kv)
    fox_lf_cache_t = jnp.swapaxes(cache_fox_logf, 2, 3)
    dsa_ik_cache_t = jnp.swapaxes(cache_dsa_idxk, 2, 3)

    bias_p, bias_s = _bias_tables(rel_bias, tq, past, n_new)

    def split(a):
        return a[:bt], a[bt:]

    def sample_seqs(a):
        return a[bt:].reshape(db, n_new, a.shape[1])

    def with_sample_rows(full, sample):
        return lax.dynamic_update_slice(full, sample.reshape(db * n_new, full.shape[1]), (bt, 0))

    outs = {k: [] for k in ("sb_p", "sb_s", "fkv_p", "fkv_s", "flf_p", "flf_s", "gs_p", "gs_s", "dkv_p", "dkv_s",
                            "dik_p", "dik_s")}
    for i in range(depth):
        j = i // 2
        g = norm_gains[i]
        x = _ffn(x, g, wg[i, 0], wu[i, 0], wd[i, 0], NG_FFN1_PRE, NG_FFN1_POST)
        if i % 2 == 0:
            w_f = ab_w_in[j][:, AB_F:]
            w_pad = _pad_cols(jnp.concatenate([ab_w_in[j], w_f, w_f], axis=1), AB_WIDTH_PAD).astype(BF16)
            b_pad = _pad_cols(jnp.tile(ab_b_forget[j], 3)[None, :], LANES)
            q_sb, kv_sb, q_fx, kv_fx, lf = _proj_ab(x, g, w_pad, b_pad)
            kv_sb_p, kv_sb_s = split(kv_sb)
            kv_fx_p, kv_fx_s = split(kv_fx)
            lf_p, lf_s = split(lf)
            cum_split = _cumsum_rows(lf, nb, t).astype(BF16)
            half = kv_sb.shape[1] // 2
            mix_a = _sb_prompt(q_sb, kv_sb[:, :half].astype(BF16), kv_sb[:, half:].T.astype(BF16), nb, t)
            mix_b = _fox_prompt(q_fx, kv_fx[:, :half].astype(BF16), kv_fx[:, half:].T.astype(BF16), cum_split,
                                nb, t)
            mix_a = with_sample_rows(mix_a, _sb_sample(page_table, sample_seqs(q_sb), sample_seqs(kv_sb),
                                                       sb_cache, j))
            mix_b = with_sample_rows(mix_b, _fox_sample(page_table, sample_seqs(q_fx), sample_seqs(kv_fx),
                                                        sample_seqs(lf), fox_cache, fox_lf_cache_t, j))
            outs["sb_p"].append(kv_sb_p.reshape(nb, t, 2, H_SB, HEAD_DIM))
            outs["sb_s"].append(kv_sb_s.reshape(db, n_new, 2, H_SB, HEAD_DIM))
            outs["fkv_p"].append(kv_fx_p.reshape(nb, t, 2, H_FOX, HEAD_DIM))
            outs["fkv_s"].append(kv_fx_s.reshape(db, n_new, 2, H_FOX, HEAD_DIM))
            outs["flf_p"].append(lf_p[:, :H_FOX].reshape(nb, t, H_FOX))
            outs["flf_s"].append(lf_s[:, :H_FOX].reshape(db, n_new, H_FOX))
        else:
            w_pad = _pad_cols(_cd_weight_layout(cd_w_in[j]), CD_WIDTH_PAD).astype(BF16)
            wgu_pad = jnp.zeros((LANES, H_GLA * DK_GLA), F32).at[MISC_GLR:MISC_GLR + GLA_RANK].set(
                gla_w_gate_up[j]).astype(BF16)
            qkv_g, r_g, q_d, kv_d, iq, ik2, misc, la = _proj_cd(x, g, w_pad, wgu_pad, gla_b_gate[j][None, :])
            gnorm = gla_norm[j][None, :]
            qkv_g_p, qkv_g_s = split(qkv_g)
            la_p, la_s = split(la)
            r_g_p, r_g_s = split(r_g)
            zero_state = jnp.zeros((nb, H_GLA * DV_GLA, H_GLA * DK_GLA), F32)
            o_g_p, s_p = _gla(qkv_g_p.reshape(nb, t, -1), la_p.reshape(nb, t, -1), r_g_p.reshape(nb, t, -1),
                              zero_state, gnorm)
            o_g_s, s_s = _gla(qkv_g_s.reshape(db, n_new, -1), la_s.reshape(db, n_new, -1),
                              r_g_s.reshape(db, n_new, -1), _state_to_block_diag(state_gla[j]), gnorm)
            kv_d_p, kv_d_s = split(kv_d)
            ik2_p, ik2_s = split(ik2)
            misc_s = misc[bt:]
            half = kv_d.shape[1] // 2
            mix_b = _dsa_prompt(rel_bias, q_d, iq, misc.T, kv_d[:, :half].astype(BF16),
                                kv_d[:, half:].T.astype(BF16), ik2.astype(BF16), bias_p, nb, t, tq)
            pad_lanes = ((0, 0), (0, 0), (0, page - n_new))
            kv_new_t = jnp.pad(jnp.swapaxes(kv_d_s.reshape(db, n_new, -1), 1, 2), pad_lanes).reshape(
                db, 2, half, page)
            ik_new_t = jnp.pad(jnp.swapaxes(ik2_s[:, :D_IDX].reshape(db, n_new, D_IDX), 1, 2), pad_lanes)
            iq_rows = iq[bt:].reshape(db, n_new * H_IDX, D_IDX)
            iw_rows = misc_s[:, MISC_IW:MISC_IW + H_IDX].reshape(db, n_new * H_IDX, 1)
            mix_b = with_sample_rows(mix_b, _dsa_sample(page_table, sample_seqs(q_d), kv_new_t, iq_rows, iw_rows,
                                                        ik_new_t, bias_s, dsa_cache, dsa_ik_cache_t, j))
            mix_a = jnp.concatenate([o_g_p.reshape(bt, -1), o_g_s.reshape(db * n_new, -1)], axis=0)
            outs["gs_p"].append(_block_diag_to_state(s_p))
            outs["gs_s"].append(_block_diag_to_state(s_s))
            outs["dkv_p"].append(kv_d_p.reshape(nb, t, 2, H_DSA, HEAD_DIM))
            outs["dkv_s"].append(kv_d_s.reshape(db, n_new, 2, H_DSA, HEAD_DIM))
            outs["dik_p"].append(ik2_p[:, :D_IDX].reshape(nb, t, D_IDX))
            outs["dik_s"].append(ik2_s[:, :D_IDX].reshape(db, n_new, D_IDX))
        x = _mixout(x, mix_a, mix_b, g, wo[i])
        x = _ffn(x, g, wg[i, 1], wu[i, 1], wd[i, 1], NG_FFN2_PRE, NG_FFN2_POST)
        x = _ple(x, p_all[i], g, wpp[i], wpg[i])

    y_p, y_s = split(x)
    return (y_p.reshape(nb, t, d), y_s.reshape(db, n_new, d),
            jnp.stack(outs["sb_p"]), jnp.stack(outs["sb_s"]),
            jnp.stack(outs["fkv_p"]), jnp.stack(outs["fkv_s"]),
            jnp.stack(outs["flf_p"]), jnp.stack(outs["flf_s"]),
            jnp.stack(outs["gs_p"]), jnp.stack(outs["gs_s"]),
            jnp.stack(outs["dkv_p"]), jnp.stack(outs["dkv_s"]),
            jnp.stack(outs["dik_p"]), jnp.stack(outs["dik_s"]))
```
